```python
import math
import jax, jax.numpy as jnp
from jax import lax
import numpy as np

D_MODEL = 1024
BATCH = 2
SEQ = 8192
DEPTH = 4
DEC_BATCH = 128
DEC_SEQ = 4
PAST_LEN = 8192
PAGE_SIZE = 128

HEAD_DIM = 64
N_HEADS = D_MODEL // HEAD_DIM
N_KV_HEADS = max(N_HEADS // 4, 1)
GROUP = N_HEADS // N_KV_HEADS
WINDOW = 128
BLOCK = WINDOW
CACHE_WIN = min(WINDOW, PAST_LEN)
ROPE_THETA = 10000.0
SCALE = HEAD_DIM ** -0.5
D_CONV = D_MODEL
CONV_W = 31
D_FF = 256 * math.ceil(8 * D_MODEL / 3 / 256)
N_EXPERTS = 8
TOP_K = 2
D_FF_E = 7 * D_MODEL // 2
N_DENSE = (DEPTH + 1) // 2
N_MOE = DEPTH // 2
Q_W = N_HEADS * HEAD_DIM
KV_W = N_KV_HEADS * HEAD_DIM
SPLITS = (Q_W, Q_W + KV_W, Q_W + 2 * KV_W, Q_W + 2 * KV_W + 2 * D_CONV, Q_W + 2 * KV_W + 2 * D_CONV + D_MODEL)
IN_W = SPLITS[-1] + D_MODEL

kernel_name = "hybrid_swa_conformer_adaln_moe_step"


def rms_norm(x, g, eps=1e-6):
    xf = x.astype(jnp.float32)
    y = xf * lax.rsqrt(jnp.mean(xf * xf, axis=-1, keepdims=True) + eps)
    return (y * g.astype(jnp.float32)).astype(x.dtype)


def layer_norm(x, g, b, eps=1e-5):
    xf = x.astype(jnp.float32)
    mu = jnp.mean(xf, axis=-1, keepdims=True)
    xc = xf - mu
    var = jnp.mean(xc * xc, axis=-1, keepdims=True)
    return (xc * lax.rsqrt(var + eps) * g.astype(jnp.float32) + b.astype(jnp.float32)).astype(x.dtype)


def rope(x, pos):
    half = HEAD_DIM // 2
    inv = ROPE_THETA ** (-jnp.arange(0, half, dtype=jnp.float32) * 2.0 / HEAD_DIM)
    ang = pos.astype(jnp.float32)[:, None] * inv[None, :]
    cos = jnp.cos(ang)[:, None, :]
    sin = jnp.sin(ang)[:, None, :]
    xf = x.astype(jnp.float32)
    x1, x2 = xf[..., :half], xf[..., half:]
    return jnp.concatenate([x1 * cos - x2 * sin, x2 * cos + x1 * sin], axis=-1).astype(x.dtype)


def ada_mod(c, w, b):
    m = jax.nn.silu(c) @ w + b
    return jnp.split(m[:, None, :], 6, axis=-1)


def modulate(h, shift, scale):
    return h * (1 + scale) + shift


def sink_softmax(s, mask, sink):
    s = jnp.where(mask, s, -jnp.inf)
    m = jnp.maximum(jnp.max(s, axis=-1, keepdims=True), sink)
    p = jnp.exp(s - m)
    return p / (jnp.sum(p, axis=-1, keepdims=True) + jnp.exp(sink - m))


def mixer_inputs(h, pos, w_in_l, qn, kn):
    B, T = h.shape[:2]
    p = h @ w_in_l
    q, k, v, u, ga, gb = jnp.split(p, SPLITS, axis=-1)
    q = rope(rms_norm(q.reshape(B, T, N_HEADS, HEAD_DIM), qn), pos)
    k = rope(rms_norm(k.reshape(B, T, N_KV_HEADS, HEAD_DIM), kn), pos)
    v = v.reshape(B, T, N_KV_HEADS, HEAD_DIM)
    ua, ug = jnp.split(u, 2, axis=-1)
    return q, k, v, ua * jax.nn.sigmoid(ug), ga, gb


def attn_prompt(q, k, v, sinks_l):
    B, T = q.shape[:2]
    nb = T // BLOCK
    qb = q.reshape(B, nb, BLOCK, N_KV_HEADS, GROUP, HEAD_DIM)
    kb = k.reshape(B, nb, BLOCK, N_KV_HEADS, HEAD_DIM)
    vb = v.reshape(B, nb, BLOCK, N_KV_HEADS, HEAD_DIM)
    pad = ((0, 0), (1, 0), (0, 0), (0, 0), (0, 0))
    kk = jnp.concatenate([jnp.pad(kb, pad)[:, :-1], kb], axis=2)
    vv = jnp.concatenate([jnp.pad(vb, pad)[:, :-1], vb], axis=2)
    s = jnp.einsum('bnqkgd,bnskd->bnkgqs', qb, kk, preferred_element_type=jnp.float32) * SCALE
    qi = jnp.arange(BLOCK)[:, None] + BLOCK
    ki = jnp.arange(2 * BLOCK)[None, :]
    delta = qi - ki
    local = (delta >= 0) & (delta < WINDOW)
    kabs = jnp.arange(nb)[:, None, None] * BLOCK - BLOCK + ki[None]
    mask = (local[None] & (kabs >= 0))[None, :, None, None]
    p = sink_softmax(s, mask, sinks_l.astype(jnp.float32).reshape(1, 1, N_KV_HEADS, GROUP, 1, 1))
    o = jnp.einsum('bnkgqs,bnskd->bnqkgd', p.astype(vv.dtype), vv)
    return o.reshape(B, T, N_HEADS * HEAD_DIM)


def attn_sample(q, kk, vv, sinks_l):
    B, S = q.shape[:2]
    L = kk.shape[1]
    qg = q.reshape(B, S, N_KV_HEADS, GROUP, HEAD_DIM)
    s = jnp.einsum('bqkgd,bskd->bkgqs', qg, kk, preferred_element_type=jnp.float32) * SCALE
    delta = jnp.arange(S)[:, None] + (L - S) - jnp.arange(L)[None, :]
    mask = (delta >= 0) & (delta < WINDOW)
    p = sink_softmax(s, mask, sinks_l.astype(jnp.float32).reshape(1, N_KV_HEADS, GROUP, 1, 1))
    o = jnp.einsum('bkgqs,bskd->bqkgd', p.astype(vv.dtype), vv)
    return o.reshape(B, S, N_HEADS * HEAD_DIM)


def depthwise_causal(zin, w, b):
    out = lax.conv_general_dilated(zin, w[:, None, :], window_strides=(1,), padding='VALID',
                                   dimension_numbers=('NWC', 'WIO', 'NWC'), feature_group_count=D_CONV)
    return out + b


def mixer_tail(o_attn, z, ga, gb, w_ao, lg, lb, w_co, w_o_l):
    a = o_attn @ w_ao
    bconv = jax.nn.silu(layer_norm(z, lg, lb)) @ w_co
    m = jax.nn.sigmoid(ga) * a + jax.nn.sigmoid(gb) * bconv
    return m @ w_o_l


def swiglu(h, wg, wu, wd):
    return (jax.nn.silu(h @ wg) * (h @ wu)) @ wd


def moe(h, w_r, b_r, wg, wu, wd):
    logits = (h @ w_r).astype(jnp.float32) + b_r.astype(jnp.float32)
    top_v, top_i = lax.top_k(logits, TOP_K)
    probs = jax.nn.softmax(top_v, axis=-1)
    comb = jnp.sum(jax.nn.one_hot(top_i, N_EXPERTS, dtype=jnp.float32) * probs[..., None], axis=-2)
    out = jnp.zeros_like(h)
    for e in range(N_EXPERTS):
        out = out + comb[..., e:e + 1].astype(h.dtype) * swiglu(h, wg[e], wu[e], wd[e])
    return out


def channel_mixer(h, l, w_ff_gate, w_ff_up, w_ff_down, w_router, b_router, w_e_gate, w_e_up, w_e_down):
    i = l // 2
    if l % 2 == 0:
        return swiglu(h, w_ff_gate[i], w_ff_up[i], w_ff_down[i])
    return moe(h, w_router[i], b_router[i], w_e_gate[i], w_e_up[i], w_e_down[i])


def setup_inputs(seed: int = 0) -> dict:
    key = jax.random.key(seed)
    ks = jax.random.split(key, 32)
    f32 = jnp.float32

    def nrm(k, shape, scale):
        return jax.random.normal(k, shape, f32) * scale

    return {
        "x_prompt": nrm(ks[0], (BATCH, SEQ, D_MODEL), 1.0),
        "x_sample": nrm(ks[1], (DEC_BATCH, DEC_SEQ, D_MODEL), 1.0),
        "cache_k_win": nrm(ks[2], (DEPTH, DEC_BATCH, CACHE_WIN, N_KV_HEADS, HEAD_DIM), 1.0),
        "cache_v_win": nrm(ks[3], (DEPTH, DEC_BATCH, CACHE_WIN, N_KV_HEADS, HEAD_DIM), 1.0),
        "state_conv": nrm(ks[4], (DEPTH, DEC_BATCH, CONV_W - 1, D_CONV), 0.5),
        "c_prompt": nrm(ks[5], (BATCH, D_MODEL), 1.0),
        "c_sample": nrm(ks[6], (DEC_BATCH, D_MODEL), 1.0),
        "w_ada": nrm(ks[7], (DEPTH, D_MODEL, 6 * D_MODEL), 0.5 * D_MODEL ** -0.5),
        "b_ada": nrm(ks[8], (DEPTH, 6 * D_MODEL), 0.02),
        "norm1": 1.0 + nrm(ks[9], (DEPTH, D_MODEL), 0.1),
        "norm2": 1.0 + nrm(ks[10], (DEPTH, D_MODEL), 0.1),
        "w_in": nrm(ks[11], (DEPTH, D_MODEL, IN_W), D_MODEL ** -0.5),
        "q_norm": 1.0 + nrm(ks[12], (DEPTH, HEAD_DIM), 0.1),
        "k_norm": 1.0 + nrm(ks[13], (DEPTH, HEAD_DIM), 0.1),
        "sinks": nrm(ks[14], (DEPTH, N_HEADS), 0.5),
        "w_attn_out": nrm(ks[15], (DEPTH, Q_W, D_MODEL), Q_W ** -0.5),
        "conv_w": nrm(ks[16], (DEPTH, CONV_W, D_CONV), CONV_W ** -0.5),
        "conv_b": nrm(ks[17], (DEPTH, D_CONV), 0.02),
        "conv_ln_g": 1.0 + nrm(ks[18], (DEPTH, D_CONV), 0.1),
        "conv_ln_b": nrm(ks[19], (DEPTH, D_CONV), 0.02),
        "w_conv_out": nrm(ks[20], (DEPTH, D_CONV, D_MODEL), D_CONV ** -0.5),
        "w_o": nrm(ks[21], (DEPTH, D_MODEL, D_MODEL), D_MODEL ** -0.5),
        "w_ff_gate": nrm(ks[22], (N_DENSE, D_MODEL, D_FF), D_MODEL ** -0.5),
        "w_ff_up": nrm(ks[23], (N_DENSE, D_MODEL, D_FF), D_MODEL ** -0.5),
        "w_ff_down": nrm(ks[24], (N_DENSE, D_FF, D_MODEL), D_FF ** -0.5),
        "w_router": nrm(ks[25], (N_MOE, D_MODEL, N_EXPERTS), D_MODEL ** -0.5),
        "b_router": nrm(ks[26], (N_MOE, N_EXPERTS), 0.01),
        "w_e_gate": nrm(ks[27], (N_MOE, N_EXPERTS, D_MODEL, D_FF_E), D_MODEL ** -0.5),
        "w_e_up": nrm(ks[28], (N_MOE, N_EXPERTS, D_MODEL, D_FF_E), D_MODEL ** -0.5),
        "w_e_down": nrm(ks[29], (N_MOE, N_EXPERTS, D_FF_E, D_MODEL), D_FF_E ** -0.5),
    }


def reference(x_prompt, x_sample, cache_k_win, cache_v_win, state_conv, c_prompt, c_sample,
              w_ada, b_ada, norm1, norm2, w_in, q_norm, k_norm, sinks, w_attn_out,
              conv_w, conv_b, conv_ln_g, conv_ln_b, w_conv_out, w_o,
              w_ff_gate, w_ff_up, w_ff_down, w_router, b_router, w_e_gate, w_e_up, w_e_down):
    seq = x_prompt.shape[1]
    prompt_win = min(WINDOW, seq)
    pos_p = jnp.arange(seq, dtype=jnp.int32)
    pos_s = PAST_LEN + jnp.arange(x_sample.shape[1], dtype=jnp.int32)
    xp, xs = x_prompt, x_sample
    nk_p, nv_p, nc_p, nk_s, nv_s, nc_s = [], [], [], [], [], []
    for l in range(DEPTH):
        mp = ada_mod(c_prompt, w_ada[l], b_ada[l])
        ms = ada_mod(c_sample, w_ada[l], b_ada[l])

        hp = modulate(rms_norm(xp, norm1[l]), mp[0], mp[1])
        q, k, v, u, ga, gb = mixer_inputs(hp, pos_p, w_in[l], q_norm[l], k_norm[l])
        o = attn_prompt(q, k, v, sinks[l])
        z = depthwise_causal(jnp.pad(u, ((0, 0), (CONV_W - 1, 0), (0, 0))), conv_w[l], conv_b[l])
        xp = xp + mp[2] * mixer_tail(o, z, ga, gb, w_attn_out[l], conv_ln_g[l], conv_ln_b[l], w_conv_out[l], w_o[l])
        nk_p.append(k[:, seq - prompt_win:])
        nv_p.append(v[:, seq - prompt_win:])
        nc_p.append(u[:, seq - (CONV_W - 1):])

        hs = modulate(rms_norm(xs, norm1[l]), ms[0], ms[1])
        q, k, v, u, ga, gb = mixer_inputs(hs, pos_s, w_in[l], q_norm[l], k_norm[l])
        kk = jnp.concatenate([cache_k_win[l], k], axis=1)
        vv = jnp.concatenate([cache_v_win[l], v], axis=1)
        o = attn_sample(q, kk, vv, sinks[l])
        zin = jnp.concatenate([state_conv[l], u], axis=1)
        z = depthwise_causal(zin, conv_w[l], conv_b[l])
        xs = xs + ms[2] * mixer_tail(o, z, ga, gb, w_attn_out[l], conv_ln_g[l], conv_ln_b[l], w_conv_out[l], w_o[l])
        nk_s.append(kk[:, kk.shape[1] - CACHE_WIN:])
        nv_s.append(vv[:, vv.shape[1] - CACHE_WIN:])
        nc_s.append(zin[:, zin.shape[1] - (CONV_W - 1):])

        hp2 = modulate(rms_norm(xp, norm2[l]), mp[3], mp[4])
        xp = xp + mp[5] * channel_mixer(hp2, l, w_ff_gate, w_ff_up, w_ff_down, w_router, b_router, w_e_gate, w_e_up, w_e_down)
        hs2 = modulate(rms_norm(xs, norm2[l]), ms[3], ms[4])
        xs = xs + ms[5] * channel_mixer(hs2, l, w_ff_gate, w_ff_up, w_ff_down, w_router, b_router, w_e_gate, w_e_up, w_e_down)

    y_prompt = xp
    y_sample = xs
    new_k_win_prompt = jnp.stack(nk_p)
    new_v_win_prompt = jnp.stack(nv_p)
    new_conv_prompt = jnp.stack(nc_p)
    new_k_win_sample = jnp.stack(nk_s)
    new_v_win_sample = jnp.stack(nv_s)
    new_conv_sample = jnp.stack(nc_s)
    return (y_prompt, y_sample, new_k_win_prompt, new_v_win_prompt, new_conv_prompt, new_k_win_sample, new_v_win_sample, new_conv_sample)
```

```python
import functools

import jax
import jax.numpy as jnp
from jax import lax
from jax.experimental import pallas as pl
from jax.experimental.pallas import tpu as pltpu

F32 = jnp.float32
BF16 = jnp.bfloat16

PAST_LEN = 8192
ROPE_THETA = 10000.0
RMS_EPS = 1e-6
LN_EPS = 1e-5
TOP_K = 2
LANES = 128
CONV_HALO = 32
VMEM_LIMIT = 56 * 1024 * 1024

NEG_INF = float("-inf")


def _params(sem, vmem=VMEM_LIMIT):
    return pltpu.CompilerParams(dimension_semantics=sem, vmem_limit_bytes=vmem)


def _mod_spec(mod, tm, rows_per_batch):
    d = mod.shape[-1]
    if mod.shape[1] == 1:
        return pl.BlockSpec((None, 1, d), lambda i, *_: ((i * tm) // rows_per_batch, 0, 0))
    return pl.BlockSpec((None, tm, d), lambda i, *_: (0, i, 0))


def _ada_kernel(c_ref, w_ref, b_ref, o_ref):
    c = c_ref[...]
    a = (c * jax.nn.sigmoid(c)).astype(BF16)
    o_ref[...] = jnp.dot(a, w_ref[...].astype(BF16), preferred_element_type=F32) + b_ref[...]


def _ada_call(c_all, w_ada, b_ada, tn=1536):
    depth, d, n = w_ada.shape
    bp = c_all.shape[0]
    return pl.pallas_call(
        _ada_kernel,
        grid=(depth, n // tn),
        in_specs=[
            pl.BlockSpec((bp, d), lambda l, j: (0, 0)),
            pl.BlockSpec((None, d, tn), lambda l, j: (l, 0, j)),
            pl.BlockSpec((None, 1, tn), lambda l, j: (l, 0, j)),
        ],
        out_specs=pl.BlockSpec((None, bp, tn), lambda l, j: (l, 0, j)),
        out_shape=jax.ShapeDtypeStruct((depth, bp, n), F32),
        compiler_params=_params(("arbitrary", "arbitrary")),
        name="ada_mod",
    )(c_all, w_ada, b_ada.reshape(depth, 1, n))


def _head_norm_rope(xc, nw, cos, sin, lo, first_half, hd):
    t = xc * xc
    s_lo = jnp.sum(jnp.where(lo, t, 0.0), axis=-1, keepdims=True)
    s_hi = jnp.sum(jnp.where(lo, 0.0, t), axis=-1, keepdims=True)
    inv = jnp.where(lo, lax.rsqrt(s_lo / hd + RMS_EPS), lax.rsqrt(s_hi / hd + RMS_EPS))
    y = xc * inv * nw
    rot = jnp.where(first_half, pltpu.roll(y, LANES - hd // 2, 1), pltpu.roll(y, hd // 2, 1))
    return y * cos + rot * sin


def _premix_kernel(x_ref, shift_ref, scale_ref, g_ref, w_ref, qn_ref, kn_ref, cos_ref, sin_ref,
                   q_ref, k_ref, v_ref, u_ref, sga_ref, sgb_ref, *, q_w, kv_w, d_conv, hd, qk_scale):
    x = x_ref[...]
    ms = jnp.mean(x * x, axis=-1, keepdims=True)
    h = x * lax.rsqrt(ms + RMS_EPS) * g_ref[...]
    h = h * (1.0 + scale_ref[...]) + shift_ref[...]
    hb = h.astype(BF16)

    lane = lax.broadcasted_iota(jnp.int32, (x.shape[0], LANES), 1)
    lo = lane < hd
    first_half = (lane % hd) < (hd // 2)
    cos = cos_ref[...]
    sin = sin_ref[...]

    def seg(start, width):
        return jnp.dot(hb, w_ref[:, start:start + width], preferred_element_type=F32)

    qf = seg(0, q_w)
    for c in range(q_w // LANES):
        out = _head_norm_rope(qf[:, c * LANES:(c + 1) * LANES], qn_ref[...], cos, sin, lo, first_half, hd)
        q_ref[:, c * LANES:(c + 1) * LANES] = (out * qk_scale).astype(BF16)
    kf = seg(q_w, kv_w)
    for c in range(kv_w // LANES):
        k_ref[:, c * LANES:(c + 1) * LANES] = _head_norm_rope(kf[:, c * LANES:(c + 1) * LANES], kn_ref[...], cos, sin, lo, first_half, hd)
    v_ref[...] = seg(q_w + kv_w, kv_w)
    o_u = q_w + 2 * kv_w
    ua = seg(o_u, d_conv)
    ug = seg(o_u + d_conv, d_conv)
    u_ref[...] = ua * jax.nn.sigmoid(ug)
    d = x.shape[1]
    sga_ref[...] = jax.nn.sigmoid(seg(o_u + 2 * d_conv, d))
    sgb_ref[...] = jax.nn.sigmoid(seg(o_u + 2 * d_conv + d, d))


def _premix_call(x, shift, scale, norm1, w_in_bf, qn2, kn2, cos, sin, l, *, rows_per_batch, tm, hd, kv_w):
    n, d = x.shape
    in_w = w_in_bf.shape[-1]
    q_w = d
    d_conv = (in_w - q_w - 2 * kv_w - 2 * d) // 2
    pos_tiles = rows_per_batch // tm
    tok = lambda w: pl.BlockSpec((tm, w), lambda i: (i, 0))
    lay = lambda a: pl.BlockSpec((None,) + a.shape[1:], lambda i: (l,) + (0,) * (a.ndim - 1))
    pos = pl.BlockSpec((tm, LANES), lambda i: (i % pos_tiles, 0))
    kern = functools.partial(_premix_kernel, q_w=q_w, kv_w=kv_w, d_conv=d_conv, hd=hd, qk_scale=hd ** -0.5)
    return pl.pallas_call(
        kern,
        grid=(n // tm,),
        in_specs=[tok(d), _mod_spec(shift, tm, rows_per_batch), _mod_spec(scale, tm, rows_per_batch),
                  lay(norm1), lay(w_in_bf), lay(qn2), lay(kn2), pos, pos],
        out_specs=[tok(q_w), tok(kv_w), tok(kv_w), tok(d_conv), tok(d), tok(d)],
        out_shape=[jax.ShapeDtypeStruct((n, q_w), BF16), jax.ShapeDtypeStruct((n, kv_w), F32),
                   jax.ShapeDtypeStruct((n, kv_w), F32), jax.ShapeDtypeStruct((n, d_conv), F32),
                   jax.ShapeDtypeStruct((n, d), F32), jax.ShapeDtypeStruct((n, d), F32)],
        compiler_params=_params(("arbitrary",)),
        name="premix",
    )(x, shift, scale, norm1, w_in_bf, qn2, kn2, cos, sin)


def _attn_prompt_kernel(sink_ref, q_ref, kp_ref, kc_ref, vp_ref, vc_ref, o_ref, *, n_heads, group, hd, blocks_per_batch):
    n = pl.program_id(0) % blocks_per_batch
    blk = q_ref.shape[0]
    qi = lax.broadcasted_iota(jnp.int32, (blk, blk), 0)
    kj = lax.broadcasted_iota(jnp.int32, (blk, blk), 1)
    mask_prev = (kj > qi) & (n > 0)
    mask_cur = kj <= qi
    dn = (((1,), (1,)), ((), ()))
    for kh in range(n_heads // group):
        cs = slice(kh * hd, (kh + 1) * hd)
        kp = kp_ref[:, cs].astype(BF16)
        kc = kc_ref[:, cs].astype(BF16)
        vp = vp_ref[:, cs].astype(BF16)
        vc = vc_ref[:, cs].astype(BF16)
        for g in range(group):
            hh = kh * group + g
            qh = q_ref[:, hh * hd:(hh + 1) * hd]
            sp = jnp.where(mask_prev, lax.dot_general(qh, kp, dn, preferred_element_type=F32), NEG_INF)
            sc = jnp.where(mask_cur, lax.dot_general(qh, kc, dn, preferred_element_type=F32), NEG_INF)
            sink = sink_ref[hh]
            m = jnp.maximum(jnp.maximum(jnp.max(sp, axis=-1, keepdims=True),
                                        jnp.max(sc, axis=-1, keepdims=True)), sink)
            pp = jnp.exp(sp - m)
            pc = jnp.exp(sc - m)
            den = jnp.sum(pp, axis=-1, keepdims=True) + jnp.sum(pc, axis=-1, keepdims=True) + jnp.exp(sink - m)
            o = (jnp.dot(pp.astype(BF16), vp, preferred_element_type=F32)
                 + jnp.dot(pc.astype(BF16), vc, preferred_element_type=F32))
            o_ref[:, hh * hd:(hh + 1) * hd] = (o / den).astype(BF16)


def _attn_prompt_call(q, k, v, sinks_l, *, rows_per_batch, blk, hd, group):
    n, q_w = q.shape
    kv_w = k.shape[1]
    bpb = rows_per_batch // blk
    cur = lambda i: (i, 0)
    prev = lambda i: (jnp.maximum(i - 1, 0), 0)
    kern = functools.partial(_attn_prompt_kernel, n_heads=q_w // hd, group=group, hd=hd, blocks_per_batch=bpb)
    return pl.pallas_call(
        kern,
        grid=(n // blk,),
        in_specs=[pl.BlockSpec(memory_space=pltpu.SMEM),
                  pl.BlockSpec((blk, q_w), cur),
                  pl.BlockSpec((blk, kv_w), prev), pl.BlockSpec((blk, kv_w), cur),
                  pl.BlockSpec((blk, kv_w), prev), pl.BlockSpec((blk, kv_w), cur)],
        out_specs=pl.BlockSpec((blk, q_w), cur),
        out_shape=jax.ShapeDtypeStruct((n, q_w), BF16),
        compiler_params=_params(("arbitrary",)),
        name="attn_prompt",
    )(sinks_l, q, k, k, v, v)


def _attn_sample_kernel(q_ref, ck_ref, cv_ref, kn_ref, vn_ref, sink_ref, o_ref, ks_ref, vs_ref, *, dec_seq, hd, n_kv):
    bb, win, kv_w = ck_ref.shape
    pad = kn_ref.shape[1]
    ks_ref[:, 0:win, :] = ck_ref[...].astype(BF16)
    vs_ref[:, 0:win, :] = cv_ref[...].astype(BF16)
    ks_ref[:, win:win + pad, :] = kn_ref[...].astype(BF16)
    vs_ref[:, win:win + pad, :] = vn_ref[...].astype(BF16)
    ks_ref[:, win + pad:, :] = jnp.zeros((bb, win - pad, kv_w), BF16)
    vs_ref[:, win + pad:, :] = jnp.zeros((bb, win - pad, kv_w), BF16)

    q = q_ref[...]
    rows = q.shape[1]
    s = jnp.einsum("bqd,bkd->bqk", q, ks_ref[...], preferred_element_type=F32)
    ti = lax.broadcasted_iota(jnp.int32, s.shape, 1) % dec_seq
    kj = lax.broadcasted_iota(jnp.int32, s.shape, 2)
    mask = ((kj < win) & (kj > ti)) | ((kj >= win) & ((kj - win) <= ti))
    s = jnp.where(mask, s, NEG_INF)
    sink = sink_ref[...][:, 0:1]
    m = jnp.maximum(jnp.max(s, axis=-1, keepdims=True), sink)
    p = jnp.exp(s - m)
    den = jnp.sum(p, axis=-1, keepdims=True) + jnp.exp(sink - m)
    o = jnp.einsum("bqk,bkd->bqd", p.astype(BF16), vs_ref[...], preferred_element_type=F32) / den
    rpk = rows // n_kv
    for kh in range(n_kv):
        o_ref[:, kh * rpk:(kh + 1) * rpk, :] = o[:, kh * rpk:(kh + 1) * rpk, kh * hd:(kh + 1) * hd].astype(BF16)


def _attn_sample_call(qblk, cache_k, cache_v, kn, vn, sink_rows, l, *, dec_seq, hd, bb=8):
    b, rows, kv_w = qblk.shape
    win = cache_k.shape[2]
    pad = kn.shape[1]
    n_kv = kv_w // hd
    per_b = lambda a: pl.BlockSpec((bb,) + a.shape[1:], lambda i: (i,) + (0,) * (a.ndim - 1))
    cache = pl.BlockSpec((None, bb, win, kv_w), lambda i: (l, i, 0, 0))
    kern = functools.partial(_attn_sample_kernel, dec_seq=dec_seq, hd=hd, n_kv=n_kv)
    return pl.pallas_call(
        kern,
        grid=(b // bb,),
        in_specs=[per_b(qblk), cache, cache, per_b(kn), per_b(vn),
                  pl.BlockSpec(sink_rows.shape, lambda i: (0, 0))],
        out_specs=pl.BlockSpec((bb, rows, hd), lambda i: (i, 0, 0)),
        out_shape=jax.ShapeDtypeStruct((b, rows, hd), BF16),
        scratch_shapes=[pltpu.VMEM((bb, 2 * win, kv_w), BF16), pltpu.VMEM((bb, 2 * win, kv_w), BF16)],
        compiler_params=_params(("arbitrary",)),
        name="attn_sample",
    )(qblk, cache_k, cache_v, kn, vn, sink_rows)


def _ln_silu(z, g, b):
    mu = jnp.mean(z, axis=-1, keepdims=True)
    zc = z - mu
    var = jnp.mean(zc * zc, axis=-1, keepdims=True)
    y = zc * lax.rsqrt(var + LN_EPS) * g + b
    return y * jax.nn.sigmoid(y)


def _conv_prompt_kernel(cur_ref, prev_ref, w_ref, b_ref, g_ref, beta_ref, o_ref, ext_ref, z_ref, *, tiles_per_batch, rc):
    tm, d = cur_ref.shape
    cw = w_ref.shape[0]
    first = (pl.program_id(0) % tiles_per_batch) == 0
    ext_ref[CONV_HALO:CONV_HALO + tm, :] = cur_ref[...]
    ext_ref[0:CONV_HALO, :] = jnp.where(first, 0.0, prev_ref[...])
    base = CONV_HALO - (cw - 1)
    for c in range(d // LANES):
        cs = slice(c * LANES, (c + 1) * LANES)
        for r in range(tm // rc):
            acc = jnp.broadcast_to(b_ref[:, cs], (rc, LANES))
            for j in range(cw):
                acc = acc + w_ref[j:j + 1, cs] * ext_ref[base + j + r * rc:base + j + (r + 1) * rc, cs]
            z_ref[r * rc:(r + 1) * rc, cs] = acc
    o_ref[...] = _ln_silu(z_ref[...], g_ref[...], beta_ref[...]).astype(BF16)


def _conv_prompt_call(u, conv_w, conv_b, ln_g, ln_b, l, *, rows_per_batch, tm, rc=64):
    n, d = u.shape
    halo_blocks = tm // CONV_HALO
    lay = lambda a: pl.BlockSpec((None,) + a.shape[1:], lambda i: (l,) + (0,) * (a.ndim - 1))
    kern = functools.partial(_conv_prompt_kernel, tiles_per_batch=rows_per_batch // tm, rc=rc)
    return pl.pallas_call(
        kern,
        grid=(n // tm,),
        in_specs=[pl.BlockSpec((tm, d), lambda i: (i, 0)),
                  pl.BlockSpec((CONV_HALO, d), lambda i: (jnp.maximum(i * halo_blocks - 1, 0), 0)),
                  lay(conv_w), lay(conv_b), lay(ln_g), lay(ln_b)],
        out_specs=pl.BlockSpec((tm, d), lambda i: (i, 0)),
        out_shape=jax.ShapeDtypeStruct((n, d), BF16),
        scratch_shapes=[pltpu.VMEM((tm + CONV_HALO, d), F32), pltpu.VMEM((tm, d), F32)],
        compiler_params=_params(("arbitrary",)),
        name="conv_prompt",
    )(u, u, conv_w, conv_b, ln_g, ln_b)


def _conv_sample_kernel(zin_ref, w_ref, b_ref, g_ref, beta_ref, o_ref, z_ref):
    bb, rows, d = o_ref.shape
    cw = w_ref.shape[0]
    for c in range(d // LANES):
        cs = slice(c * LANES, (c + 1) * LANES)
        acc = jnp.broadcast_to(b_ref[:, cs], (bb, rows, LANES))
        for j in range(cw):
            acc = acc + w_ref[j:j + 1, cs] * zin_ref[:, j:j + rows, cs]
        z_ref[:, :, cs] = acc
    o_ref[...] = _ln_silu(z_ref[...], g_ref[...], beta_ref[...]).astype(BF16)


def _conv_sample_call(zin, conv_w, conv_b, ln_g, ln_b, l, *, rows, bb=8):
    b, ext, d = zin.shape
    lay = lambda a: pl.BlockSpec((None,) + a.shape[1:], lambda i: (l,) + (0,) * (a.ndim - 1))
    return pl.pallas_call(
        _conv_sample_kernel,
        grid=(b // bb,),
        in_specs=[pl.BlockSpec((bb, ext, d), lambda i: (i, 0, 0)),
                  lay(conv_w), lay(conv_b), lay(ln_g), lay(ln_b)],
        out_specs=pl.BlockSpec((bb, rows, d), lambda i: (i, 0, 0)),
        out_shape=jax.ShapeDtypeStruct((b, rows, d), BF16),
        scratch_shapes=[pltpu.VMEM((bb, rows, d), F32)],
        compiler_params=_params(("arbitrary",)),
        name="conv_sample",
    )(zin, conv_w, conv_b, ln_g, ln_b)


def _tail_kernel(*refs, route, n_exp):
    if route:
        (o_ref, c_ref, sga_ref, sgb_ref, x_ref, gate_ref, shift_ref, scale_ref, g2_ref,
         wao_ref, wco_ref, wo_ref, wr_ref, br_ref, xo_ref, h2_ref, rt_ref) = refs
    else:
        (o_ref, c_ref, sga_ref, sgb_ref, x_ref, gate_ref, shift_ref, scale_ref, g2_ref,
         wao_ref, wco_ref, wo_ref, xo_ref, h2_ref) = refs
    a = jnp.dot(o_ref[...], wao_ref[...], preferred_element_type=F32)
    bc = jnp.dot(c_ref[...], wco_ref[...], preferred_element_type=F32)
    m = sga_ref[...] * a + sgb_ref[...] * bc
    y = jnp.dot(m.astype(BF16), wo_ref[...], preferred_element_type=F32)
    x = x_ref[...] + gate_ref[...] * y
    xo_ref[...] = x
    ms = jnp.mean(x * x, axis=-1, keepdims=True)
    h2 = x * lax.rsqrt(ms + RMS_EPS) * g2_ref[...]
    h2 = h2 * (1.0 + scale_ref[...]) + shift_ref[...]
    h2_ref[...] = h2
    if route:
        logits = jnp.dot(h2, wr_ref[...], preferred_element_type=F32, precision=lax.Precision.HIGHEST)
        lane = lax.broadcasted_iota(jnp.int32, logits.shape, 1)
        lanef = lane.astype(F32)
        lg = jnp.where(lane < n_exp, logits + br_ref[...], NEG_INF)
        v1 = jnp.max(lg, axis=-1, keepdims=True)
        i1 = jnp.min(jnp.where(lg == v1, lanef, float(LANES)), axis=-1, keepdims=True)
        lg2 = jnp.where(lanef == i1, NEG_INF, lg)
        v2 = jnp.max(lg2, axis=-1, keepdims=True)
        i2 = jnp.min(jnp.where(lg2 == v2, lanef, float(LANES)), axis=-1, keepdims=True)
        e = jnp.exp(v2 - v1)
        p1 = 1.0 / (1.0 + e)
        p2 = e * p1
        rt_ref[...] = jnp.where(lane == 0, i1, jnp.where(lane == 1, i2, jnp.where(lane == 2, p1, jnp.where(lane == 3, p2, 0.0))))


def _tail_call(o, c, sga, sgb, x, gate, shift, scale, norm2, w_ao, w_co, w_o, l, router, *, n_exp, rows_per_batch, tm):
    n, d = x.shape
    tok = lambda w: pl.BlockSpec((tm, w), lambda i: (i, 0))
    lay = lambda a, ll: pl.BlockSpec((None,) + a.shape[1:], lambda i: (ll,) + (0,) * (a.ndim - 1))
    mod = lambda a: _mod_spec(a, tm, rows_per_batch)
    in_specs = [tok(d), tok(d), tok(d), tok(d), tok(d), mod(gate), mod(shift), mod(scale), lay(norm2, l),
                lay(w_ao, l), lay(w_co, l), lay(w_o, l)]
    args = [o, c, sga, sgb, x, gate, shift, scale, norm2, w_ao, w_co, w_o]
    out_specs = [tok(d), tok(d)]
    out_shape = [jax.ShapeDtypeStruct((n, d), F32), jax.ShapeDtypeStruct((n, d), F32)]
    if router is not None:
        w_r, b_r, li = router
        in_specs += [lay(w_r, li), lay(b_r, li)]
        args += [w_r, b_r]
        out_specs.append(tok(LANES))
        out_shape.append(jax.ShapeDtypeStruct((n, LANES), F32))
    return pl.pallas_call(
        functools.partial(_tail_kernel, route=router is not None, n_exp=n_exp),
        grid=(n // tm,),
        in_specs=in_specs, out_specs=out_specs, out_shape=out_shape,
        compiler_params=_params(("arbitrary",)),
        name="mixer_tail",
    )(*args)


def _swiglu_step(hb_ref, wg_ref, wu_ref, wd_ref, acc_ref):
    hb = hb_ref[...]
    g = jnp.dot(hb, wg_ref[...], preferred_element_type=F32)
    u = jnp.dot(hb, wu_ref[...], preferred_element_type=F32)
    a = (g * jax.nn.sigmoid(g)) * u
    acc_ref[...] += jnp.dot(a.astype(BF16), wd_ref[...], preferred_element_type=F32)


def _ffn_dense_kernel(h_ref, x_ref, gate_ref, wg_ref, wu_ref, wd_ref, o_ref, hb_ref, acc_ref):
    j = pl.program_id(1)

    @pl.when(j == 0)
    def _():
        hb_ref[...] = h_ref[...].astype(BF16)
        acc_ref[...] = jnp.zeros_like(acc_ref)

    _swiglu_step(hb_ref, wg_ref, wu_ref, wd_ref, acc_ref)

    @pl.when(j == pl.num_programs(1) - 1)
    def _():
        o_ref[...] = x_ref[...] + gate_ref[...] * acc_ref[...]


def _ffn_dense_call(h2, x, gate, wg, wu, wd, li, *, rows_per_batch, tm, tf):
    n, d = x.shape
    f = wg.shape[-1]
    tok = pl.BlockSpec((tm, d), lambda i, j: (i, 0))
    return pl.pallas_call(
        _ffn_dense_kernel,
        grid=(n // tm, f // tf),
        in_specs=[tok, tok, _mod_spec(gate, tm, rows_per_batch),
                  pl.BlockSpec((None, d, tf), lambda i, j: (li, 0, j)),
                  pl.BlockSpec((None, d, tf), lambda i, j: (li, 0, j)),
                  pl.BlockSpec((None, tf, d), lambda i, j: (li, j, 0))],
        out_specs=tok,
        out_shape=jax.ShapeDtypeStruct((n, d), F32),
        scratch_shapes=[pltpu.VMEM((tm, d), BF16), pltpu.VMEM((tm, d), F32)],
        compiler_params=_params(("arbitrary", "arbitrary")),
        name="ffn_dense",
    )(h2, x, gate, wg, wu, wd)


def _ffn_group_kernel(te_ref, na_ref, xs_ref, wg_ref, wu_ref, wd_ref, y_ref, hb_ref, acc_ref):
    m = pl.program_id(0)
    j = pl.program_id(1)

    @pl.when(m < na_ref[0])
    def _():
        @pl.when(j == 0)
        def _():
            hb_ref[...] = xs_ref[...].astype(BF16)
            acc_ref[...] = jnp.zeros_like(acc_ref)

        _swiglu_step(hb_ref, wg_ref, wu_ref, wd_ref, acc_ref)

    @pl.when(j == pl.num_programs(1) - 1)
    def _():
        y_ref[...] = jnp.where(m < na_ref[0], acc_ref[...], 0.0)


def _ffn_group_call(tile_expert, n_active, xs, wg, wu, wd, li, *, tm, tf):
    r, d = xs.shape
    f = wg.shape[-1]
    tok = pl.BlockSpec((tm, d), lambda m, j, te, na: (m, 0))
    grid_spec = pltpu.PrefetchScalarGridSpec(
        num_scalar_prefetch=2,
        grid=(r // tm, f // tf),
        in_specs=[tok,
                  pl.BlockSpec((None, None, d, tf), lambda m, j, te, na: (li, te[m], 0, j)),
                  pl.BlockSpec((None, None, d, tf), lambda m, j, te, na: (li, te[m], 0, j)),
                  pl.BlockSpec((None, None, tf, d), lambda m, j, te, na: (li, te[m], j, 0))],
        out_specs=tok,
        scratch_shapes=[pltpu.VMEM((tm, d), BF16), pltpu.VMEM((tm, d), F32)],
    )
    return pl.pallas_call(
        _ffn_group_kernel,
        grid_spec=grid_spec,
        out_shape=jax.ShapeDtypeStruct((r, d), F32),
        compiler_params=_params(("arbitrary", "arbitrary")),
        name="ffn_grouped",
    )(tile_expert, n_active, xs, wg, wu, wd)


def _row_copy(src_hbm, row, dst_ref, slot, sem):
    return pltpu.make_async_copy(src_hbm.at[pl.ds(row, 1), :], dst_ref.at[pl.ds(slot, 1), :], sem)


def _gather_kernel(idx_ref, src_hbm, o_ref, sem):
    tm = o_ref.shape[0]
    base = pl.program_id(0) * tm

    def start(r, carry):
        _row_copy(src_hbm, idx_ref[base + r], o_ref, r, sem).start()
        return carry

    def wait(r, carry):
        _row_copy(src_hbm, 0, o_ref, r, sem).wait()
        return carry

    lax.fori_loop(0, tm, start, 0)
    lax.fori_loop(0, tm, wait, 0)


def _gather_call(src_idx, h_all, *, tm):
    r = src_idx.shape[0]
    d = h_all.shape[1]
    grid_spec = pltpu.PrefetchScalarGridSpec(
        num_scalar_prefetch=1,
        grid=(r // tm,),
        in_specs=[pl.BlockSpec(memory_space=pl.ANY)],
        out_specs=pl.BlockSpec((tm, d), lambda i, idx: (i, 0)),
        scratch_shapes=[pltpu.SemaphoreType.DMA(())],
    )
    return pl.pallas_call(
        _gather_kernel,
        grid_spec=grid_spec,
        out_shape=jax.ShapeDtypeStruct((r, d), h_all.dtype),
        compiler_params=_params(("arbitrary",)),
        name="moe_gather",
    )(src_idx, h_all)


def _combine_kernel(pos_ref, x_ref, gate_ref, rt_ref, y_hbm, o_ref, buf_ref, sem, *, tok_off, n_tok):
    tm = o_ref.shape[0]
    base = tok_off + pl.program_id(0) * tm

    def start(r, carry):
        for s in range(TOP_K):
            _row_copy(y_hbm, pos_ref[s * n_tok + base + r], buf_ref.at[s], r, sem).start()
        return carry

    def wait(r, carry):
        for s in range(TOP_K):
            _row_copy(y_hbm, 0, buf_ref.at[s], r, sem).wait()
        return carry

    lax.fori_loop(0, tm, start, 0)
    lax.fori_loop(0, tm, wait, 0)
    rt = rt_ref[...]
    mix = rt[:, 2:3] * buf_ref[0] + rt[:, 3:4] * buf_ref[1]
    o_ref[...] = x_ref[...] + gate_ref[...] * mix


def _combine_call(pos, x, gate, route, y_sorted, *, tok_off, n_tok, rows_per_batch, tm):
    n, d = x.shape
    tok = lambda w: pl.BlockSpec((tm, w), lambda i, p: (i, 0))
    grid_spec = pltpu.PrefetchScalarGridSpec(
        num_scalar_prefetch=1,
        grid=(n // tm,),
        in_specs=[tok(d), _mod_spec(gate, tm, rows_per_batch), tok(LANES), pl.BlockSpec(memory_space=pl.ANY)],
        out_specs=tok(d),
        scratch_shapes=[pltpu.VMEM((TOP_K, tm, d), F32), pltpu.SemaphoreType.DMA(())],
    )
    return pl.pallas_call(
        functools.partial(_combine_kernel, tok_off=tok_off, n_tok=n_tok),
        grid_spec=grid_spec,
        out_shape=jax.ShapeDtypeStruct((n, d), F32),
        compiler_params=_params(("arbitrary",)),
        name="moe_combine",
    )(pos, x, gate, route, y_sorted)


def _routing_plan(route_all, n_exp, tm):
    n_tok = route_all.shape[0]
    e = jnp.concatenate([route_all[:, 0], route_all[:, 1]]).astype(jnp.int32)
    onehot = (e[:, None] == jnp.arange(n_exp, dtype=jnp.int32)[None, :]).astype(jnp.int32)
    cnt = jnp.sum(onehot, axis=0)
    rank = jnp.sum((jnp.cumsum(onehot, axis=0) - onehot) * onehot, axis=1)
    padded = ((cnt + tm - 1) // tm) * tm
    ends = jnp.cumsum(padded)
    pos = (ends - padded)[e] + rank
    n_rows = TOP_K * n_tok + n_exp * tm
    tok_id = jnp.tile(jnp.arange(n_tok, dtype=jnp.int32), TOP_K)
    src = jnp.zeros((n_rows,), jnp.int32).at[pos].set(tok_id)
    tile_start = jnp.arange(n_rows // tm, dtype=jnp.int32) * tm
    tile_expert = jnp.minimum(jnp.searchsorted(ends, tile_start, side="right"), n_exp - 1).astype(jnp.int32)
    n_active = (ends[-1] // tm).astype(jnp.int32).reshape(1)
    return pos.astype(jnp.int32), src, tile_expert, n_active


def _rope_tables(pos, hd):
    half = hd // 2
    inv = ROPE_THETA ** (-jnp.arange(0, half, dtype=F32) * 2.0 / hd)
    ang = pos.astype(F32)[:, None] * inv[None, :]
    cos = jnp.cos(ang)
    sin = jnp.sin(ang)
    reps = LANES // hd
    return jnp.tile(jnp.concatenate([cos, cos], axis=1), (1, reps)), jnp.tile(jnp.concatenate([-sin, sin], axis=1), (1, reps))


def kernel(x_prompt, x_sample, cache_k_win, cache_v_win, state_conv, c_prompt, c_sample, w_ada, b_ada, norm1, norm2, w_in, q_norm, k_norm, sinks, w_attn_out, conv_w, conv_b, conv_ln_g, conv_ln_b, w_conv_out, w_o, w_ff_gate, w_ff_up, w_ff_down, w_router, b_router, w_e_gate, w_e_up, w_e_down):
    bsz, seq, d = x_prompt.shape
    dec_b, dec_s, _ = x_sample.shape
    depth = w_in.shape[0]
    hd = q_norm.shape[1]
    n_heads = sinks.shape[1]
    win, n_kv = cache_k_win.shape[2], cache_k_win.shape[3]
    kv_w = n_kv * hd
    group = n_heads // n_kv
    cw = conv_w.shape[1]
    n_exp = w_router.shape[-1]
    np_tok, ns_tok = bsz * seq, dec_b * dec_s
    n_tok = np_tok + ns_tok

    c_all = jnp.concatenate([c_prompt, c_sample], axis=0)
    bp = -(-c_all.shape[0] // 8) * 8
    mods = _ada_call(jnp.pad(c_all, ((0, bp - c_all.shape[0]), (0, 0))), w_ada, b_ada)
    mods = mods.reshape(depth, bp, 6, d)
    mod_p = [[mods[l, :bsz, i][:, None, :] for i in range(6)] for l in range(depth)]
    mod_s = [[jnp.repeat(mods[l, bsz:bsz + dec_b, i], dec_s, axis=0)[None] for i in range(6)] for l in range(depth)]

    w_in_bf, w_ao_bf, w_co_bf, w_o_bf = (w.astype(BF16) for w in (w_in, w_attn_out, w_conv_out, w_o))
    w_fg_bf, w_fu_bf, w_fd_bf = (w.astype(BF16) for w in (w_ff_gate, w_ff_up, w_ff_down))
    w_eg_bf, w_eu_bf, w_ed_bf = (w.astype(BF16) for w in (w_e_gate, w_e_up, w_e_down))
    w_r_pad = jnp.pad(w_router, ((0, 0), (0, 0), (0, LANES - n_exp)))
    b_r3 = jnp.pad(b_router, ((0, 0), (0, LANES - n_exp))).reshape(-1, 1, LANES)

    reps = LANES // hd
    qn2 = jnp.tile(q_norm, (1, reps)).reshape(depth, 1, LANES)
    kn2 = jnp.tile(k_norm, (1, reps)).reshape(depth, 1, LANES)
    cos_p, sin_p = _rope_tables(jnp.arange(seq, dtype=jnp.int32), hd)
    cos_s, sin_s = _rope_tables(jnp.tile(PAST_LEN + jnp.arange(dec_s, dtype=jnp.int32), dec_b), hd)
    r3 = lambda a: a.reshape(depth, 1, a.shape[-1])
    norm1_3, norm2_3, conv_b3, ln_g3, ln_b3 = r3(norm1), r3(norm2), r3(conv_b), r3(conv_ln_g), r3(conv_ln_b)
    cache_k = cache_k_win.reshape(depth, dec_b, win, kv_w)
    cache_v = cache_v_win.reshape(depth, dec_b, win, kv_w)
    head_kv = (jnp.arange(n_heads) // group)
    blockdiag = (head_kv[:, None] == jnp.arange(n_kv)[None, :]).astype(BF16)
    new_pad = 16

    xp = x_prompt.reshape(np_tok, d)
    xs = x_sample.reshape(ns_tok, d)
    outs = {k: [] for k in ("kp", "vp", "cp", "ks", "vs", "cs")}
    tm_moe = 512

    for l in range(depth):
        mp, ms = mod_p[l], mod_s[l]
        q, k, v, u, sga, sgb = _premix_call(xp, mp[0], mp[1], norm1_3, w_in_bf, qn2, kn2, cos_p, sin_p, l,
                                            rows_per_batch=seq, tm=256, hd=hd, kv_w=kv_w)
        o = _attn_prompt_call(q, k, v, sinks[l], rows_per_batch=seq, blk=win, hd=hd, group=group)
        cv = _conv_prompt_call(u, conv_w, conv_b3, ln_g3, ln_b3, l, rows_per_batch=seq, tm=256)
        outs["kp"].append(k.reshape(bsz, seq, n_kv, hd)[:, seq - win:])
        outs["vp"].append(v.reshape(bsz, seq, n_kv, hd)[:, seq - win:])
        outs["cp"].append(u.reshape(bsz, seq, d)[:, seq - (cw - 1):])
        moe_layer = l % 2 == 1
        router = (w_r_pad, b_r3, l // 2) if moe_layer else None
        res_p = _tail_call(o, cv, sga, sgb, xp, mp[2], mp[3], mp[4], norm2_3, w_ao_bf, w_co_bf, w_o_bf, l, router,
                           n_exp=n_exp, rows_per_batch=seq, tm=256)

        q, k, v, u, sga, sgb = _premix_call(xs, ms[0], ms[1], norm1_3, w_in_bf, qn2, kn2, cos_s, sin_s, l,
                                            rows_per_batch=ns_tok, tm=256, hd=hd, kv_w=kv_w)
        q4 = q.reshape(dec_b, dec_s, n_heads, hd).transpose(0, 2, 1, 3)
        qblk = (q4[:, :, :, None, :] * blockdiag[None, :, None, :, None]).reshape(dec_b, n_heads * dec_s, kv_w)
        k3, v3 = k.reshape(dec_b, dec_s, kv_w), v.reshape(dec_b, dec_s, kv_w)
        padn = lambda a: jnp.pad(a, ((0, 0), (0, new_pad - dec_s), (0, 0)))
        sink_rows = jnp.broadcast_to(jnp.repeat(sinks[l], dec_s)[:, None], (n_heads * dec_s, LANES))
        o = _attn_sample_call(qblk, cache_k, cache_v, padn(k3), padn(v3), sink_rows, l, dec_seq=dec_s, hd=hd)
        o = o.reshape(dec_b, n_heads, dec_s, hd).transpose(0, 2, 1, 3).reshape(ns_tok, n_heads * hd)
        u3 = u.reshape(dec_b, dec_s, d)
        zin = jnp.concatenate([state_conv[l], u3], axis=1)
        rows = 8
        zin_pad = jnp.pad(zin, ((0, 0), (0, rows + cw - 1 - zin.shape[1]), (0, 0)))
        cv = _conv_sample_call(zin_pad, conv_w, conv_b3, ln_g3, ln_b3, l, rows=rows)[:, :dec_s].reshape(ns_tok, d)
        outs["ks"].append(jnp.concatenate([cache_k[l], k3], axis=1)[:, dec_s:].reshape(dec_b, win, n_kv, hd))
        outs["vs"].append(jnp.concatenate([cache_v[l], v3], axis=1)[:, dec_s:].reshape(dec_b, win, n_kv, hd))
        outs["cs"].append(zin[:, zin.shape[1] - (cw - 1):])
        res_s = _tail_call(o, cv, sga, sgb, xs, ms[2], ms[3], ms[4], norm2_3, w_ao_bf, w_co_bf, w_o_bf, l, router,
                           n_exp=n_exp, rows_per_batch=ns_tok, tm=256)

        if not moe_layer:
            xp, h2p = res_p
            xs, h2s = res_s
            xp = _ffn_dense_call(h2p, xp, mp[5], w_fg_bf, w_fu_bf, w_fd_bf, l // 2, rows_per_batch=seq, tm=512, tf=1408)
            xs = _ffn_dense_call(h2s, xs, ms[5], w_fg_bf, w_fu_bf, w_fd_bf, l // 2, rows_per_batch=ns_tok, tm=512, tf=1408)
        else:
            xp, h2p, rtp = res_p
            xs, h2s, rts = res_s
            h_all = jnp.concatenate([h2p, h2s], axis=0)
            rt_all = jnp.concatenate([rtp, rts], axis=0)
            pos, src, tile_expert, n_active = _routing_plan(rt_all, n_exp, tm_moe)
            xg = _gather_call(src, h_all, tm=tm_moe)
            y = _ffn_group_call(tile_expert, n_active, xg, w_eg_bf, w_eu_bf, w_ed_bf, l // 2, tm=tm_moe, tf=896)
            xp = _combine_call(pos, xp, mp[5], rtp, y, tok_off=0, n_tok=n_tok, rows_per_batch=seq, tm=256)
            xs = _combine_call(pos, xs, ms[5], rts, y, tok_off=np_tok, n_tok=n_tok, rows_per_batch=ns_tok, tm=256)

    return (xp.reshape(bsz, seq, d), xs.reshape(dec_b, dec_s, d),
            jnp.stack(outs["kp"]), jnp.stack(outs["vp"]), jnp.stack(outs["cp"]),
            jnp.stack(outs["ks"]), jnp.stack(outs["vs"]), jnp.stack(outs["cs"]))
```

```python
import functools

import jax
import jax.numpy as jnp
from jax import lax
from jax.experimental import pallas as pl
from jax.experimental.pallas import tpu as pltpu

F32 = jnp.float32
BF16 = jnp.bfloat16

PAST_LEN = 8192
ROPE_THETA = 10000.0
RMS_EPS = 1e-6
LN_EPS = 1e-5
TOP_K = 2
LANES = 128
SUBLANES = 8
CONV_HALO = 32
VMEM_LIMIT = 56 * 1024 * 1024
DMA_UNROLL = 8

NEG_INF = float("-inf")


def _params(sem, vmem=VMEM_LIMIT):
    return pltpu.CompilerParams(dimension_semantics=sem, vmem_limit_bytes=vmem)


def _mod_spec(mod, tm, rows_per_batch):
    d = mod.shape[-1]
    if mod.shape[1] == 1:
        return pl.BlockSpec((None, 1, d), lambda i, *_: ((i * tm) // rows_per_batch, 0, 0))
    return pl.BlockSpec((None, tm, d), lambda i, *_: (0, i, 0))


def _ada_kernel(c_ref, w_ref, b_ref, o_ref):
    c = c_ref[...]
    a = (c * jax.nn.sigmoid(c)).astype(BF16)
    o_ref[...] = jnp.dot(a, w_ref[...].astype(BF16), preferred_element_type=F32) + b_ref[...]


def _ada_call(c_all, w_ada, b_ada, tn=1536):
    depth, d, n = w_ada.shape
    bp = c_all.shape[0]
    return pl.pallas_call(
        _ada_kernel,
        grid=(depth, n // tn),
        in_specs=[
            pl.BlockSpec((bp, d), lambda l, j: (0, 0)),
            pl.BlockSpec((None, d, tn), lambda l, j: (l, 0, j)),
            pl.BlockSpec((None, 1, tn), lambda l, j: (l, 0, j)),
        ],
        out_specs=pl.BlockSpec((None, bp, tn), lambda l, j: (l, 0, j)),
        out_shape=jax.ShapeDtypeStruct((depth, bp, n), F32),
        compiler_params=_params(("arbitrary", "arbitrary")),
        name="ada_mod",
    )(c_all, w_ada, b_ada.reshape(depth, 1, n))


def _head_norm_rope(xc, nw, cos, sin, lo, first_half, hd):
    t = xc * xc
    s_lo = jnp.sum(jnp.where(lo, t, 0.0), axis=-1, keepdims=True)
    s_hi = jnp.sum(jnp.where(lo, 0.0, t), axis=-1, keepdims=True)
    inv = jnp.where(lo, lax.rsqrt(s_lo / hd + RMS_EPS), lax.rsqrt(s_hi / hd + RMS_EPS))
    y = xc * inv * nw
    rot = jnp.where(first_half, pltpu.roll(y, LANES - hd // 2, 1), pltpu.roll(y, hd // 2, 1))
    return y * cos + rot * sin


def _premix_kernel(x_ref, shift_ref, scale_ref, g_ref, w_ref, qn_ref, kn_ref, cos_ref, sin_ref,
                   q_ref, k_ref, v_ref, u_ref, sga_ref, sgb_ref, *, q_w, kv_w, d_conv, hd, qk_scale):
    x = x_ref[...]
    ms = jnp.mean(x * x, axis=-1, keepdims=True)
    h = x * lax.rsqrt(ms + RMS_EPS) * g_ref[...]
    h = h * (1.0 + scale_ref[...]) + shift_ref[...]
    hb = h.astype(BF16)

    lane = lax.broadcasted_iota(jnp.int32, (x.shape[0], LANES), 1)
    lo = lane < hd
    first_half = (lane % hd) < (hd // 2)
    cos = cos_ref[...]
    sin = sin_ref[...]

    def seg(start, width):
        return jnp.dot(hb, w_ref[:, start:start + width], preferred_element_type=F32)

    qf = seg(0, q_w)
    for c in range(q_w // LANES):
        out = _head_norm_rope(qf[:, c * LANES:(c + 1) * LANES], qn_ref[...], cos, sin, lo, first_half, hd)
        q_ref[:, c * LANES:(c + 1) * LANES] = (out * qk_scale).astype(BF16)
    kf = seg(q_w, kv_w)
    for c in range(kv_w // LANES):
        k_ref[:, c * LANES:(c + 1) * LANES] = _head_norm_rope(kf[:, c * LANES:(c + 1) * LANES], kn_ref[...], cos, sin, lo, first_half, hd)
    v_ref[...] = seg(q_w + kv_w, kv_w)
    o_u = q_w + 2 * kv_w
    ua = seg(o_u, d_conv)
    ug = seg(o_u + d_conv, d_conv)
    u_ref[...] = ua * jax.nn.sigmoid(ug)
    d = x.shape[1]
    sga_ref[...] = jax.nn.sigmoid(seg(o_u + 2 * d_conv, d))
    sgb_ref[...] = jax.nn.sigmoid(seg(o_u + 2 * d_conv + d, d))


def _premix_call(x, shift, scale, norm1, w_in_bf, qn2, kn2, cos, sin, l, *, rows_per_batch, tm, hd, kv_w):
    n, d = x.shape
    in_w = w_in_bf.shape[-1]
    q_w = d
    d_conv = (in_w - q_w - 2 * kv_w - 2 * d) // 2
    pos_tiles = rows_per_batch // tm
    tok = lambda w: pl.BlockSpec((tm, w), lambda i: (i, 0))
    lay = lambda a: pl.BlockSpec((None,) + a.shape[1:], lambda i: (l,) + (0,) * (a.ndim - 1))
    pos = pl.BlockSpec((tm, LANES), lambda i: (i % pos_tiles, 0))
    kern = functools.partial(_premix_kernel, q_w=q_w, kv_w=kv_w, d_conv=d_conv, hd=hd, qk_scale=hd ** -0.5)
    return pl.pallas_call(
        kern,
        grid=(n // tm,),
        in_specs=[tok(d), _mod_spec(shift, tm, rows_per_batch), _mod_spec(scale, tm, rows_per_batch),
                  lay(norm1), lay(w_in_bf), lay(qn2), lay(kn2), pos, pos],
        out_specs=[tok(q_w), tok(kv_w), tok(kv_w), tok(d_conv), tok(d), tok(d)],
        out_shape=[jax.ShapeDtypeStruct((n, q_w), BF16), jax.ShapeDtypeStruct((n, kv_w), F32),
                   jax.ShapeDtypeStruct((n, kv_w), F32), jax.ShapeDtypeStruct((n, d_conv), F32),
                   jax.ShapeDtypeStruct((n, d), F32), jax.ShapeDtypeStruct((n, d), F32)],
        compiler_params=_params(("arbitrary",)),
        name="premix",
    )(x, shift, scale, norm1, w_in_bf, qn2, kn2, cos, sin)


def _attn_prompt_kernel(sink_ref, q_ref, kp_ref, kc_ref, vp_ref, vc_ref, o_ref, *, n_heads, group, hd, blocks_per_batch):
    n = pl.program_id(0) % blocks_per_batch
    blk = q_ref.shape[0]
    qi = lax.broadcasted_iota(jnp.int32, (blk, 2 * blk), 0)
    cj = lax.broadcasted_iota(jnp.int32, (blk, 2 * blk), 1)
    valid = ((cj < blk) & (cj > qi) & (n > 0)) | ((cj >= blk) & ((cj - blk) <= qi))
    bias = jnp.where(valid, 0.0, NEG_INF)
    lane = lax.broadcasted_iota(jnp.int32, (blk, LANES), 1)
    lo = lane < hd
    dn = (((1,), (1,)), ((), ()))
    pairs_per_kv = group // 2
    for kh in range(n_heads // group):
        csl = slice((kh // 2) * LANES, (kh // 2 + 1) * LANES)
        keep = lo if kh % 2 == 0 else jnp.logical_not(lo)

        def both_offsets(ref):
            own = jnp.where(keep, ref[:, csl], 0.0)
            swapped = pltpu.roll(own, hd, 1)
            at0, at1 = (own, swapped) if kh % 2 == 0 else (swapped, own)
            return at0.astype(BF16), at1.astype(BF16)

        kp0, kp1 = both_offsets(kp_ref)
        kc0, kc1 = both_offsets(kc_ref)
        vp0, vp1 = both_offsets(vp_ref)
        vc0, vc1 = both_offsets(vc_ref)
        kcat = jnp.concatenate([kp0, kc0, kp1, kc1], axis=0)
        vcat = jnp.concatenate([vp0, vc0, vp1, vc1], axis=0)
        chunk0 = kh * pairs_per_kv
        q2 = jnp.concatenate([q_ref[:, (chunk0 + rp) * LANES:(chunk0 + rp + 1) * LANES] for rp in range(pairs_per_kv)], axis=0)
        s = lax.dot_general(q2, kcat, dn, preferred_element_type=F32)
        p_rows, inv_rows = [], []
        for rp in range(pairs_per_kv):
            p_halves, invs = [], []
            for hp in range(2):
                sink = sink_ref[kh * group + 2 * rp + hp]
                sq = s[rp * blk:(rp + 1) * blk, hp * 2 * blk:(hp + 1) * 2 * blk] + bias
                m = jnp.maximum(jnp.max(sq, axis=-1, keepdims=True), sink)
                p = jnp.exp(sq - m)
                den = jnp.sum(p, axis=-1, keepdims=True) + jnp.exp(sink - m)
                p_halves.append(p.astype(BF16))
                invs.append(1.0 / den)
            p_rows.append(jnp.concatenate(p_halves, axis=1))
            inv_rows.append(jnp.where(lo, invs[0], invs[1]))
        o2 = jnp.dot(jnp.concatenate(p_rows, axis=0), vcat, preferred_element_type=F32)
        for rp in range(pairs_per_kv):
            o_ref[:, (chunk0 + rp) * LANES:(chunk0 + rp + 1) * LANES] = (o2[rp * blk:(rp + 1) * blk] * inv_rows[rp]).astype(BF16)


def _attn_prompt_call(q, k, v, sinks_l, *, rows_per_batch, blk, hd, group):
    n, q_w = q.shape
    kv_w = k.shape[1]
    bpb = rows_per_batch // blk
    cur = lambda i: (i, 0)
    prev = lambda i: (jnp.maximum(i - 1, 0), 0)
    kern = functools.partial(_attn_prompt_kernel, n_heads=q_w // hd, group=group, hd=hd, blocks_per_batch=bpb)
    return pl.pallas_call(
        kern,
        grid=(n // blk,),
        in_specs=[pl.BlockSpec(memory_space=pltpu.SMEM),
                  pl.BlockSpec((blk, q_w), cur),
                  pl.BlockSpec((blk, kv_w), prev), pl.BlockSpec((blk, kv_w), cur),
                  pl.BlockSpec((blk, kv_w), prev), pl.BlockSpec((blk, kv_w), cur)],
        out_specs=pl.BlockSpec((blk, q_w), cur),
        out_shape=jax.ShapeDtypeStruct((n, q_w), BF16),
        compiler_params=_params(("arbitrary",)),
        name="attn_prompt",
    )(sinks_l, q, k, k, v, v)


def _attn_sample_kernel(q_ref, ck_ref, cv_ref, kn_ref, vn_ref, sink_ref, o_ref, ks_ref, vs_ref, *, dec_seq, hd, n_kv):
    bb, win, kv_w = ck_ref.shape
    pad = kn_ref.shape[1]
    ks_ref[:, 0:win, :] = ck_ref[...].astype(BF16)
    vs_ref[:, 0:win, :] = cv_ref[...].astype(BF16)
    ks_ref[:, win:win + pad, :] = kn_ref[...].astype(BF16)
    vs_ref[:, win:win + pad, :] = vn_ref[...].astype(BF16)
    ks_ref[:, win + pad:, :] = jnp.zeros((bb, win - pad, kv_w), BF16)
    vs_ref[:, win + pad:, :] = jnp.zeros((bb, win - pad, kv_w), BF16)

    q = q_ref[...]
    rows = q.shape[1]
    s = jnp.einsum("bqd,bkd->bqk", q, ks_ref[...], preferred_element_type=F32)
    ti = lax.broadcasted_iota(jnp.int32, s.shape, 1) % dec_seq
    kj = lax.broadcasted_iota(jnp.int32, s.shape, 2)
    mask = ((kj < win) & (kj > ti)) | ((kj >= win) & ((kj - win) <= ti))
    s = jnp.where(mask, s, NEG_INF)
    sink = sink_ref[...][:, 0:1]
    m = jnp.maximum(jnp.max(s, axis=-1, keepdims=True), sink)
    p = jnp.exp(s - m)
    den = jnp.sum(p, axis=-1, keepdims=True) + jnp.exp(sink - m)
    o = jnp.einsum("bqk,bkd->bqd", p.astype(BF16), vs_ref[...], preferred_element_type=F32) / den
    rpk = rows // n_kv
    for kh in range(n_kv):
        o_ref[:, kh * rpk:(kh + 1) * rpk, :] = o[:, kh * rpk:(kh + 1) * rpk, kh * hd:(kh + 1) * hd].astype(BF16)


def _attn_sample_call(qblk, cache_k, cache_v, kn, vn, sink_rows, l, *, dec_seq, hd, bb=8):
    b, rows, kv_w = qblk.shape
    win = cache_k.shape[2]
    pad = kn.shape[1]
    n_kv = kv_w // hd
    per_b = lambda a: pl.BlockSpec((bb,) + a.shape[1:], lambda i: (i,) + (0,) * (a.ndim - 1))
    cache = pl.BlockSpec((None, bb, win, kv_w), lambda i: (l, i, 0, 0))
    kern = functools.partial(_attn_sample_kernel, dec_seq=dec_seq, hd=hd, n_kv=n_kv)
    return pl.pallas_call(
        kern,
        grid=(b // bb,),
        in_specs=[per_b(qblk), cache, cache, per_b(kn), per_b(vn),
                  pl.BlockSpec(sink_rows.shape, lambda i: (0, 0))],
        out_specs=pl.BlockSpec((bb, rows, hd), lambda i: (i, 0, 0)),
        out_shape=jax.ShapeDtypeStruct((b, rows, hd), BF16),
        scratch_shapes=[pltpu.VMEM((bb, 2 * win, kv_w), BF16), pltpu.VMEM((bb, 2 * win, kv_w), BF16)],
        compiler_params=_params(("arbitrary",)),
        name="attn_sample",
    )(qblk, cache_k, cache_v, kn, vn, sink_rows)


def _ln_silu(z, g, b):
    mu = jnp.mean(z, axis=-1, keepdims=True)
    zc = z - mu
    var = jnp.mean(zc * zc, axis=-1, keepdims=True)
    y = zc * lax.rsqrt(var + LN_EPS) * g + b
    return y * jax.nn.sigmoid(y)


def _conv_prompt_kernel(cur_ref, prev_ref, w_ref, b_ref, g_ref, beta_ref, o_ref, ext_ref, sh_ref, z_ref, *, tiles_per_batch, rc):
    tm, d = cur_ref.shape
    cw = w_ref.shape[0]
    first = (pl.program_id(0) % tiles_per_batch) == 0
    ext_ref[CONV_HALO:CONV_HALO + tm, :] = cur_ref[...]
    ext_ref[0:CONV_HALO, :] = jnp.where(first, 0.0, prev_ref[...])
    ls = sh_ref.shape[1]
    for s in range(1, SUBLANES):
        sh_ref[s - 1] = ext_ref[s:s + ls, :]
    base = CONV_HALO - (cw - 1)
    for c in range(d // LANES):
        cs = slice(c * LANES, (c + 1) * LANES)
        for r in range(tm // rc):
            acc = jnp.broadcast_to(b_ref[:, cs], (rc, LANES))
            for j in range(cw):
                a, s = divmod(base + j, SUBLANES)
                r0 = a * SUBLANES + r * rc
                src = ext_ref[r0:r0 + rc, cs] if s == 0 else sh_ref[s - 1, r0:r0 + rc, cs]
                acc = acc + w_ref[j:j + 1, cs] * src
            z_ref[r * rc:(r + 1) * rc, cs] = acc
    o_ref[...] = _ln_silu(z_ref[...], g_ref[...], beta_ref[...]).astype(BF16)


def _conv_prompt_call(u, conv_w, conv_b, ln_g, ln_b, l, *, rows_per_batch, tm, rc=64):
    n, d = u.shape
    halo_blocks = tm // CONV_HALO
    lay = lambda a: pl.BlockSpec((None,) + a.shape[1:], lambda i: (l,) + (0,) * (a.ndim - 1))
    kern = functools.partial(_conv_prompt_kernel, tiles_per_batch=rows_per_batch // tm, rc=rc)
    return pl.pallas_call(
        kern,
        grid=(n // tm,),
        in_specs=[pl.BlockSpec((tm, d), lambda i: (i, 0)),
                  pl.BlockSpec((CONV_HALO, d), lambda i: (jnp.maximum(i * halo_blocks - 1, 0), 0)),
                  lay(conv_w), lay(conv_b), lay(ln_g), lay(ln_b)],
        out_specs=pl.BlockSpec((tm, d), lambda i: (i, 0)),
        out_shape=jax.ShapeDtypeStruct((n, d), BF16),
        scratch_shapes=[pltpu.VMEM((tm + CONV_HALO, d), F32),
                        pltpu.VMEM((SUBLANES - 1, tm + CONV_HALO - SUBLANES, d), F32),
                        pltpu.VMEM((tm, d), F32)],
        compiler_params=_params(("arbitrary",)),
        name="conv_prompt",
    )(u, u, conv_w, conv_b, ln_g, ln_b)


def _conv_sample_kernel(zin_ref, w_ref, b_ref, g_ref, beta_ref, o_ref, z_ref):
    bb, rows, d = o_ref.shape
    cw = w_ref.shape[0]
    for c in range(d // LANES):
        cs = slice(c * LANES, (c + 1) * LANES)
        acc = jnp.broadcast_to(b_ref[:, cs], (bb, rows, LANES))
        for j in range(cw):
            acc = acc + w_ref[j:j + 1, cs] * zin_ref[:, j:j + rows, cs]
        z_ref[:, :, cs] = acc
    o_ref[...] = _ln_silu(z_ref[...], g_ref[...], beta_ref[...]).astype(BF16)


def _conv_sample_call(zin, conv_w, conv_b, ln_g, ln_b, l, *, rows, bb=8):
    b, ext, d = zin.shape
    lay = lambda a: pl.BlockSpec((None,) + a.shape[1:], lambda i: (l,) + (0,) * (a.ndim - 1))
    return pl.pallas_call(
        _conv_sample_kernel,
        grid=(b // bb,),
        in_specs=[pl.BlockSpec((bb, ext, d), lambda i: (i, 0, 0)),
                  lay(conv_w), lay(conv_b), lay(ln_g), lay(ln_b)],
        out_specs=pl.BlockSpec((bb, rows, d), lambda i: (i, 0, 0)),
        out_shape=jax.ShapeDtypeStruct((b, rows, d), BF16),
        scratch_shapes=[pltpu.VMEM((bb, rows, d), F32)],
        compiler_params=_params(("arbitrary",)),
        name="conv_sample",
    )(zin, conv_w, conv_b, ln_g, ln_b)


def _tail_kernel(*refs, route, n_exp):
    if route:
        (o_ref, c_ref, sga_ref, sgb_ref, x_ref, gate_ref, shift_ref, scale_ref, g2_ref,
         wao_ref, wco_ref, wo_ref, wr_ref, br_ref, xo_ref, h2_ref, rt_ref) = refs
    else:
        (o_ref, c_ref, sga_ref, sgb_ref, x_ref, gate_ref, shift_ref, scale_ref, g2_ref,
         wao_ref, wco_ref, wo_ref, xo_ref, h2_ref) = refs
    a = jnp.dot(o_ref[...], wao_ref[...], preferred_element_type=F32)
    bc = jnp.dot(c_ref[...], wco_ref[...], preferred_element_type=F32)
    m = sga_ref[...] * a + sgb_ref[...] * bc
    y = jnp.dot(m.astype(BF16), wo_ref[...], preferred_element_type=F32)
    x = x_ref[...] + gate_ref[...] * y
    xo_ref[...] = x
    ms = jnp.mean(x * x, axis=-1, keepdims=True)
    h2 = x * lax.rsqrt(ms + RMS_EPS) * g2_ref[...]
    h2 = h2 * (1.0 + scale_ref[...]) + shift_ref[...]
    h2_ref[...] = h2
    if route:
        h_hi = h2.astype(BF16)
        h_lo = (h2 - h_hi.astype(F32)).astype(BF16)
        r = (jnp.dot(h_hi, wr_ref[...], preferred_element_type=F32)
             + jnp.dot(h_lo, wr_ref[...], preferred_element_type=F32))
        logits = r + pltpu.roll(r, LANES - n_exp, 1)
        lane = lax.broadcasted_iota(jnp.int32, logits.shape, 1)
        lanef = lane.astype(F32)
        lg = jnp.where(lane < n_exp, logits + br_ref[...], NEG_INF)
        v1 = jnp.max(lg, axis=-1, keepdims=True)
        i1 = jnp.min(jnp.where(lg == v1, lanef, float(LANES)), axis=-1, keepdims=True)
        lg2 = jnp.where(lanef == i1, NEG_INF, lg)
        v2 = jnp.max(lg2, axis=-1, keepdims=True)
        i2 = jnp.min(jnp.where(lg2 == v2, lanef, float(LANES)), axis=-1, keepdims=True)
        e = jnp.exp(v2 - v1)
        p1 = 1.0 / (1.0 + e)
        p2 = e * p1
        rt_ref[...] = jnp.where(lane == 0, i1, jnp.where(lane == 1, i2, jnp.where(lane == 2, p1, jnp.where(lane == 3, p2, 0.0))))


def _tail_call(o, c, sga, sgb, x, gate, shift, scale, norm2, w_ao, w_co, w_o, l, router, *, n_exp, rows_per_batch, tm):
    n, d = x.shape
    tok = lambda w: pl.BlockSpec((tm, w), lambda i: (i, 0))
    lay = lambda a, ll: pl.BlockSpec((None,) + a.shape[1:], lambda i: (ll,) + (0,) * (a.ndim - 1))
    mod = lambda a: _mod_spec(a, tm, rows_per_batch)
    in_specs = [tok(d), tok(d), tok(d), tok(d), tok(d), mod(gate), mod(shift), mod(scale), lay(norm2, l),
                lay(w_ao, l), lay(w_co, l), lay(w_o, l)]
    args = [o, c, sga, sgb, x, gate, shift, scale, norm2, w_ao, w_co, w_o]
    out_specs = [tok(d), tok(d)]
    out_shape = [jax.ShapeDtypeStruct((n, d), F32), jax.ShapeDtypeStruct((n, d), F32)]
    if router is not None:
        w_r, b_r, li = router
        in_specs += [lay(w_r, li), lay(b_r, li)]
        args += [w_r, b_r]
        out_specs.append(tok(LANES))
        out_shape.append(jax.ShapeDtypeStruct((n, LANES), F32))
    return pl.pallas_call(
        functools.partial(_tail_kernel, route=router is not None, n_exp=n_exp),
        grid=(n // tm,),
        in_specs=in_specs, out_specs=out_specs, out_shape=out_shape,
        compiler_params=_params(("arbitrary",)),
        name="mixer_tail",
    )(*args)


def _swiglu_step(hb_ref, wg_ref, wu_ref, wd_ref, acc_ref):
    hb = hb_ref[...]
    g = jnp.dot(hb, wg_ref[...], preferred_element_type=F32)
    u = jnp.dot(hb, wu_ref[...], preferred_element_type=F32)
    a = (g * jax.nn.sigmoid(g)) * u
    acc_ref[...] += jnp.dot(a.astype(BF16), wd_ref[...], preferred_element_type=F32)


def _ffn_dense_kernel(h_ref, x_ref, gate_ref, wg_ref, wu_ref, wd_ref, o_ref, hb_ref, acc_ref):
    j = pl.program_id(1)

    @pl.when(j == 0)
    def _():
        hb_ref[...] = h_ref[...].astype(BF16)
        acc_ref[...] = jnp.zeros_like(acc_ref)

    _swiglu_step(hb_ref, wg_ref, wu_ref, wd_ref, acc_ref)

    @pl.when(j == pl.num_programs(1) - 1)
    def _():
        o_ref[...] = x_ref[...] + gate_ref[...] * acc_ref[...]


def _ffn_dense_call(h2, x, gate, wg, wu, wd, li, *, rows_per_batch, tm, tf):
    n, d = x.shape
    f = wg.shape[-1]
    tok = pl.BlockSpec((tm, d), lambda i, j: (i, 0))
    return pl.pallas_call(
        _ffn_dense_kernel,
        grid=(n // tm, f // tf),
        in_specs=[tok, tok, _mod_spec(gate, tm, rows_per_batch),
                  pl.BlockSpec((None, d, tf), lambda i, j: (li, 0, j)),
                  pl.BlockSpec((None, d, tf), lambda i, j: (li, 0, j)),
                  pl.BlockSpec((None, tf, d), lambda i, j: (li, j, 0))],
        out_specs=tok,
        out_shape=jax.ShapeDtypeStruct((n, d), F32),
        scratch_shapes=[pltpu.VMEM((tm, d), BF16), pltpu.VMEM((tm, d), F32)],
        compiler_params=_params(("arbitrary", "arbitrary")),
        name="ffn_dense",
    )(h2, x, gate, wg, wu, wd)


def _ffn_group_kernel(te_ref, na_ref, nv_ref, src_ref, nsrc_ref, dst_ref, h_hbm, wg_ref, wu_ref, wd_ref, y_hbm,
                      xbuf, hb_ref, acc_ref, ybuf, gsem, ssem):
    m = pl.program_id(0)
    j = pl.program_id(1)
    na = na_ref[0]
    tm = hb_ref.shape[0]
    slot = m % 2

    def gather(idx_ref, sl):
        def body(rb, carry):
            for u in range(DMA_UNROLL):
                r = rb * DMA_UNROLL + u
                pltpu.make_async_copy(h_hbm.at[pl.ds(idx_ref[0, r], 1), :], xbuf.at[sl, pl.ds(r, 1), :],
                                      gsem.at[sl]).start(priority=u % 2)
            return carry
        lax.fori_loop(0, tm // DMA_UNROLL, body, 0)

    def gather_wait(sl):
        def body(rb, carry):
            for u in range(DMA_UNROLL):
                pltpu.make_async_copy(h_hbm.at[pl.ds(0, 1), :], xbuf.at[sl, pl.ds(rb * DMA_UNROLL + u, 1), :],
                                      gsem.at[sl]).wait()
            return carry
        lax.fori_loop(0, tm // DMA_UNROLL, body, 0)

    def scatter_row(r, row):
        return pltpu.make_async_copy(ybuf.at[pl.ds(r, 1), :], y_hbm.at[pl.ds(row, 1), :], ssem)

    def scatter_wait(count):
        def body(r, carry):
            scatter_row(r, 0).wait()
            return carry
        lax.fori_loop(0, count, body, 0)

    @pl.when(m < na)
    def _():
        @pl.when(j == 0)
        def _():
            @pl.when(m == 0)
            def _():
                gather(src_ref, 0)

            @pl.when(m + 1 < na)
            def _():
                gather(nsrc_ref, 1 - slot)

            gather_wait(slot)
            hb_ref[...] = xbuf[slot].astype(BF16)
            acc_ref[...] = jnp.zeros_like(acc_ref)

        _swiglu_step(hb_ref, wg_ref, wu_ref, wd_ref, acc_ref)

        @pl.when(j == pl.num_programs(1) - 1)
        def _():
            @pl.when(m > 0)
            def _():
                scatter_wait(nv_ref[jnp.maximum(m - 1, 0)])

            ybuf[...] = acc_ref[...]

            def body(r, carry):
                scatter_row(r, dst_ref[0, r]).start()
                return carry
            lax.fori_loop(0, nv_ref[m], body, 0)

            @pl.when(m == na - 1)
            def _():
                scatter_wait(nv_ref[m])


def _ffn_group_call(tile_expert, n_active, n_valid, src, dst, h_all, wg, wu, wd, li, *, tm, tf):
    n_tiles = tile_expert.shape[0]
    d = h_all.shape[1]
    f = wg.shape[-1]
    idx = lambda fn: pl.BlockSpec((None, 1, tm), lambda m, j, te, na, nv: (fn(m), 0, 0), memory_space=pltpu.SMEM)
    wspec = lambda shape, fn: pl.BlockSpec((None, None) + shape, lambda m, j, te, na, nv: (li, te[m]) + fn(j))
    grid_spec = pltpu.PrefetchScalarGridSpec(
        num_scalar_prefetch=3,
        grid=(n_tiles, f // tf),
        in_specs=[idx(lambda m: m), idx(lambda m: jnp.minimum(m + 1, n_tiles - 1)), idx(lambda m: m),
                  pl.BlockSpec(memory_space=pl.ANY),
                  wspec((d, tf), lambda j: (0, j)), wspec((d, tf), lambda j: (0, j)), wspec((tf, d), lambda j: (j, 0))],
        out_specs=pl.BlockSpec(memory_space=pl.ANY),
        scratch_shapes=[pltpu.VMEM((2, tm, d), F32), pltpu.VMEM((tm, d), BF16), pltpu.VMEM((tm, d), F32),
                        pltpu.VMEM((tm, d), F32), pltpu.SemaphoreType.DMA((2,)), pltpu.SemaphoreType.DMA(())],
    )
    src3 = src.reshape(n_tiles, 1, tm)
    return pl.pallas_call(
        _ffn_group_kernel,
        grid_spec=grid_spec,
        out_shape=jax.ShapeDtypeStruct((TOP_K * (h_all.shape[0]), d), F32),
        compiler_params=_params(("arbitrary", "arbitrary")),
        name="ffn_grouped",
    )(tile_expert, n_active, n_valid, src3, src3, dst.reshape(n_tiles, 1, tm), h_all, wg, wu, wd)


def _combine_kernel(x_ref, gate_ref, rt_ref, y1_ref, y2_ref, o_ref):
    rt = rt_ref[...]
    o_ref[...] = x_ref[...] + gate_ref[...] * (rt[:, 2:3] * y1_ref[...] + rt[:, 3:4] * y2_ref[...])


def _combine_call(x, gate, route, y_slots, *, tok_off, rows_per_batch, tm):
    n, d = x.shape
    tok = lambda w: pl.BlockSpec((tm, w), lambda i: (i, 0))
    slot = lambda s: pl.BlockSpec((None, tm, d), lambda i: (s, tok_off // tm + i, 0))
    return pl.pallas_call(
        _combine_kernel,
        grid=(n // tm,),
        in_specs=[tok(d), _mod_spec(gate, tm, rows_per_batch), tok(LANES), slot(0), slot(1)],
        out_specs=tok(d),
        out_shape=jax.ShapeDtypeStruct((n, d), F32),
        compiler_params=_params(("arbitrary",)),
        name="moe_combine",
    )(x, gate, route, y_slots, y_slots)


def _routing_plan(route_all, n_exp, tm):
    n_tok = route_all.shape[0]
    e = jnp.concatenate([route_all[:, 0], route_all[:, 1]]).astype(jnp.int32)
    onehot = (e[:, None] == jnp.arange(n_exp, dtype=jnp.int32)[None, :]).astype(jnp.int32)
    cnt = jnp.sum(onehot, axis=0)
    rank = jnp.sum((jnp.cumsum(onehot, axis=0) - onehot) * onehot, axis=1)
    padded = ((cnt + tm - 1) // tm) * tm
    ends = jnp.cumsum(padded)
    starts = ends - padded
    pos = starts[e] + rank
    n_rows = TOP_K * n_tok + n_exp * tm
    pair = jnp.arange(TOP_K * n_tok, dtype=jnp.int32)
    dst = jnp.zeros((n_rows,), jnp.int32).at[pos].set(pair)
    src = dst % n_tok
    tile_start = jnp.arange(n_rows // tm, dtype=jnp.int32) * tm
    tile_expert = jnp.minimum(jnp.searchsorted(ends, tile_start, side="right"), n_exp - 1).astype(jnp.int32)
    n_valid = jnp.clip((starts + cnt)[tile_expert] - tile_start, 0, tm).astype(jnp.int32)
    n_active = (ends[-1] // tm).astype(jnp.int32).reshape(1)
    return src, dst, tile_expert, n_valid, n_active


def _rope_tables(pos, hd):
    half = hd // 2
    inv = ROPE_THETA ** (-jnp.arange(0, half, dtype=F32) * 2.0 / hd)
    ang = pos.astype(F32)[:, None] * inv[None, :]
    cos = jnp.cos(ang)
    sin = jnp.sin(ang)
    reps = LANES // hd
    return jnp.tile(jnp.concatenate([cos, cos], axis=1), (1, reps)), jnp.tile(jnp.concatenate([-sin, sin], axis=1), (1, reps))


def kernel(x_prompt, x_sample, cache_k_win, cache_v_win, state_conv, c_prompt, c_sample, w_ada, b_ada, norm1, norm2, w_in, q_norm, k_norm, sinks, w_attn_out, conv_w, conv_b, conv_ln_g, conv_ln_b, w_conv_out, w_o, w_ff_gate, w_ff_up, w_ff_down, w_router, b_router, w_e_gate, w_e_up, w_e_down):
    bsz, seq, d = x_prompt.shape
    dec_b, dec_s, _ = x_sample.shape
    depth = w_in.shape[0]
    hd = q_norm.shape[1]
    n_heads = sinks.shape[1]
    win, n_kv = cache_k_win.shape[2], cache_k_win.shape[3]
    kv_w = n_kv * hd
    group = n_heads // n_kv
    cw = conv_w.shape[1]
    n_exp = w_router.shape[-1]
    np_tok, ns_tok = bsz * seq, dec_b * dec_s
    n_tok = np_tok + ns_tok

    c_all = jnp.concatenate([c_prompt, c_sample], axis=0)
    bp = -(-c_all.shape[0] // 8) * 8
    mods = _ada_call(jnp.pad(c_all, ((0, bp - c_all.shape[0]), (0, 0))), w_ada, b_ada)
    mods = mods.reshape(depth, bp, 6, d)
    mod_p = [[mods[l, :bsz, i][:, None, :] for i in range(6)] for l in range(depth)]
    mod_s = [[jnp.repeat(mods[l, bsz:bsz + dec_b, i], dec_s, axis=0)[None] for i in range(6)] for l in range(depth)]

    w_in_bf, w_ao_bf, w_co_bf, w_o_bf = (w.astype(BF16) for w in (w_in, w_attn_out, w_conv_out, w_o))
    w_fg_bf, w_fu_bf, w_fd_bf = (w.astype(BF16) for w in (w_ff_gate, w_ff_up, w_ff_down))
    w_eg_bf, w_eu_bf, w_ed_bf = (w.astype(BF16) for w in (w_e_gate, w_e_up, w_e_down))
    w_r_hi = w_router.astype(BF16)
    w_r_lo = (w_router - w_r_hi.astype(F32)).astype(BF16)
    w_r_pad = jnp.pad(jnp.concatenate([w_r_hi, w_r_lo], axis=-1), ((0, 0), (0, 0), (0, LANES - 2 * n_exp)))
    b_r3 = jnp.pad(b_router, ((0, 0), (0, LANES - n_exp))).reshape(-1, 1, LANES)

    reps = LANES // hd
    qn2 = jnp.tile(q_norm, (1, reps)).reshape(depth, 1, LANES)
    kn2 = jnp.tile(k_norm, (1, reps)).reshape(depth, 1, LANES)
    cos_p, sin_p = _rope_tables(jnp.arange(seq, dtype=jnp.int32), hd)
    cos_s, sin_s = _rope_tables(jnp.tile(PAST_LEN + jnp.arange(dec_s, dtype=jnp.int32), dec_b), hd)
    r3 = lambda a: a.reshape(depth, 1, a.shape[-1])
    norm1_3, norm2_3, conv_b3, ln_g3, ln_b3 = r3(norm1), r3(norm2), r3(conv_b), r3(conv_ln_g), r3(conv_ln_b)
    cache_k = cache_k_win.reshape(depth, dec_b, win, kv_w)
    cache_v = cache_v_win.reshape(depth, dec_b, win, kv_w)
    head_kv = (jnp.arange(n_heads) // group)
    blockdiag = (head_kv[:, None] == jnp.arange(n_kv)[None, :]).astype(BF16)
    new_pad = 16

    xp = x_prompt.reshape(np_tok, d)
    xs = x_sample.reshape(ns_tok, d)
    outs = {k: [] for k in ("kp", "vp", "cp", "ks", "vs", "cs")}
    tm_moe = 512

    for l in range(depth):
        mp, ms = mod_p[l], mod_s[l]
        q, k, v, u, sga, sgb = _premix_call(xp, mp[0], mp[1], norm1_3, w_in_bf, qn2, kn2, cos_p, sin_p, l,
                                            rows_per_batch=seq, tm=256, hd=hd, kv_w=kv_w)
        o = _attn_prompt_call(q, k, v, sinks[l], rows_per_batch=seq, blk=win, hd=hd, group=group)
        cv = _conv_prompt_call(u, conv_w, conv_b3, ln_g3, ln_b3, l, rows_per_batch=seq, tm=256)
        outs["kp"].append(k.reshape(bsz, seq, kv_w)[:, seq - win:].reshape(bsz, win, n_kv, hd))
        outs["vp"].append(v.reshape(bsz, seq, kv_w)[:, seq - win:].reshape(bsz, win, n_kv, hd))
        outs["cp"].append(u.reshape(bsz, seq, d)[:, seq - (cw - 1):])
        moe_layer = l % 2 == 1
        router = (w_r_pad, b_r3, l // 2) if moe_layer else None
        res_p = _tail_call(o, cv, sga, sgb, xp, mp[2], mp[3], mp[4], norm2_3, w_ao_bf, w_co_bf, w_o_bf, l, router,
                           n_exp=n_exp, rows_per_batch=seq, tm=256)

        q, k, v, u, sga, sgb = _premix_call(xs, ms[0], ms[1], norm1_3, w_in_bf, qn2, kn2, cos_s, sin_s, l,
                                            rows_per_batch=ns_tok, tm=256, hd=hd, kv_w=kv_w)
        q4 = q.reshape(dec_b, dec_s, n_heads, hd).transpose(0, 2, 1, 3)
        qblk = (q4[:, :, :, None, :] * blockdiag[None, :, None, :, None]).reshape(dec_b, n_heads * dec_s, kv_w)
        k3, v3 = k.reshape(dec_b, dec_s, kv_w), v.reshape(dec_b, dec_s, kv_w)
        padn = lambda a: jnp.pad(a, ((0, 0), (0, new_pad - dec_s), (0, 0)))
        sink_rows = jnp.broadcast_to(jnp.repeat(sinks[l], dec_s)[:, None], (n_heads * dec_s, LANES))
        o = _attn_sample_call(qblk, cache_k, cache_v, padn(k3), padn(v3), sink_rows, l, dec_seq=dec_s, hd=hd)
        o = o.reshape(dec_b, n_heads, dec_s, hd).transpose(0, 2, 1, 3).reshape(ns_tok, n_heads * hd)
        u3 = u.reshape(dec_b, dec_s, d)
        zin = jnp.concatenate([state_conv[l], u3], axis=1)
        rows = 8
        zin_pad = jnp.pad(zin, ((0, 0), (0, rows + cw - 1 - zin.shape[1]), (0, 0)))
        cv = _conv_sample_call(zin_pad, conv_w, conv_b3, ln_g3, ln_b3, l, rows=rows)[:, :dec_s].reshape(ns_tok, d)
        outs["ks"].append(jnp.concatenate([cache_k[l], k3], axis=1)[:, dec_s:].reshape(dec_b, win, n_kv, hd))
        outs["vs"].append(jnp.concatenate([cache_v[l], v3], axis=1)[:, dec_s:].reshape(dec_b, win, n_kv, hd))
        outs["cs"].append(zin[:, zin.shape[1] - (cw - 1):])
        res_s = _tail_call(o, cv, sga, sgb, xs, ms[2], ms[3], ms[4], norm2_3, w_ao_bf, w_co_bf, w_o_bf, l, router,
                           n_exp=n_exp, rows_per_batch=ns_tok, tm=256)

        if not moe_layer:
            xp, h2p = res_p
            xs, h2s = res_s
            xp = _ffn_dense_call(h2p, xp, mp[5], w_fg_bf, w_fu_bf, w_fd_bf, l // 2, rows_per_batch=seq, tm=512, tf=1408)
            xs = _ffn_dense_call(h2s, xs, ms[5], w_fg_bf, w_fu_bf, w_fd_bf, l // 2, rows_per_batch=ns_tok, tm=512, tf=1408)
        else:
            xp, h2p, rtp = res_p
            xs, h2s, rts = res_s
            h_all = jnp.concatenate([h2p, h2s], axis=0)
            rt_all = jnp.concatenate([rtp, rts], axis=0)
            src, dst, tile_expert, n_valid, n_active = _routing_plan(rt_all, n_exp, tm_moe)
            y = _ffn_group_call(tile_expert, n_active, n_valid, src, dst, h_all, w_eg_bf, w_eu_bf, w_ed_bf, l // 2,
                                tm=tm_moe, tf=896).reshape(TOP_K, n_tok, d)
            xp = _combine_call(xp, mp[5], rtp, y, tok_off=0, rows_per_batch=seq, tm=256)
            xs = _combine_call(xs, ms[5], rts, y, tok_off=np_tok, rows_per_batch=ns_tok, tm=256)

    return (xp.reshape(bsz, seq, d), xs.reshape(dec_b, dec_s, d),
            jnp.stack(outs["kp"]), jnp.stack(outs["vp"]), jnp.stack(outs["cp"]),
            jnp.stack(outs["ks"]), jnp.stack(outs["vs"]), jnp.stack(outs["cs"]))
```

```python
import functools

import jax
import jax.numpy as jnp
from jax import lax
from jax.experimental import pallas as pl
from jax.experimental.pallas import tpu as pltpu

F32 = jnp.float32
BF16 = jnp.bfloat16

PAST_LEN = 8192
ROPE_THETA = 10000.0
RMS_EPS = 1e-6
LN_EPS = 1e-5
TOP_K = 2
LANES = 128
SUBLANES = 8
CONV_HALO = 32
VMEM_LIMIT = 56 * 1024 * 1024
DMA_UNROLL = 8

NEG_INF = float("-inf")


def _params(sem, vmem=VMEM_LIMIT):
    return pltpu.CompilerParams(dimension_semantics=sem, vmem_limit_bytes=vmem)


def _mod_spec(mod, tm, rows_per_batch):
    arr, l, which = mod
    d = arr.shape[-1]
    if arr.shape[3] == 1:
        return pl.BlockSpec((None, None, None, 1, d), lambda i, *_: (l, which, (i * tm) // rows_per_batch, 0, 0))
    return pl.BlockSpec((None, None, None, tm, d), lambda i, *_: (l, which, 0, i, 0))


def _ada_kernel(c_ref, w_ref, b_ref, o_ref):
    c = c_ref[...]
    a = (c * jax.nn.sigmoid(c)).astype(BF16)
    o_ref[...] = jnp.dot(a, w_ref[...].astype(BF16), preferred_element_type=F32) + b_ref[...]


def _ada_call(c_all, w_ada, b_ada, tn=1536):
    depth, d, n = w_ada.shape
    bp = c_all.shape[0]
    return pl.pallas_call(
        _ada_kernel,
        grid=(depth, n // tn),
        in_specs=[
            pl.BlockSpec((bp, d), lambda l, j: (0, 0)),
            pl.BlockSpec((None, d, tn), lambda l, j: (l, 0, j)),
            pl.BlockSpec((None, 1, tn), lambda l, j: (l, 0, j)),
        ],
        out_specs=pl.BlockSpec((None, bp, tn), lambda l, j: (l, 0, j)),
        out_shape=jax.ShapeDtypeStruct((depth, bp, n), F32),
        compiler_params=_params(("arbitrary", "arbitrary")),
        name="ada_mod",
    )(c_all, w_ada, b_ada.reshape(depth, 1, n))


def _head_norm_rope(xc, nw, cos, sin, lo, first_half, hd):
    t = xc * xc
    s_lo = jnp.sum(jnp.where(lo, t, 0.0), axis=-1, keepdims=True)
    s_hi = jnp.sum(jnp.where(lo, 0.0, t), axis=-1, keepdims=True)
    inv = jnp.where(lo, lax.rsqrt(s_lo / hd + RMS_EPS), lax.rsqrt(s_hi / hd + RMS_EPS))
    y = xc * inv * nw
    rot = jnp.where(first_half, pltpu.roll(y, LANES - hd // 2, 1), pltpu.roll(y, hd // 2, 1))
    return y * cos + rot * sin


def _premix_kernel(x_ref, shift_ref, scale_ref, g_ref, w_ref, qn_ref, kn_ref, cos_ref, sin_ref,
                   q_ref, k_ref, v_ref, u_ref, sga_ref, sgb_ref, *, q_w, kv_w, d_conv, hd, qk_scale):
    x = x_ref[...]
    ms = jnp.mean(x * x, axis=-1, keepdims=True)
    h = x * lax.rsqrt(ms + RMS_EPS) * g_ref[...]
    h = h * (1.0 + scale_ref[...]) + shift_ref[...]
    hb = h.astype(BF16)

    lane = lax.broadcasted_iota(jnp.int32, (x.shape[0], LANES), 1)
    lo = lane < hd
    first_half = (lane % hd) < (hd // 2)
    cos = cos_ref[...]
    sin = sin_ref[...]

    def seg(start, width):
        return jnp.dot(hb, w_ref[:, start:start + width], preferred_element_type=F32)

    qf = seg(0, q_w)
    for c in range(q_w // LANES):
        out = _head_norm_rope(qf[:, c * LANES:(c + 1) * LANES], qn_ref[...], cos, sin, lo, first_half, hd)
        q_ref[:, c * LANES:(c + 1) * LANES] = (out * qk_scale).astype(BF16)
    kf = seg(q_w, kv_w)
    for c in range(kv_w // LANES):
        k_ref[:, c * LANES:(c + 1) * LANES] = _head_norm_rope(kf[:, c * LANES:(c + 1) * LANES], kn_ref[...], cos, sin, lo, first_half, hd)
    v_ref[...] = seg(q_w + kv_w, kv_w)
    o_u = q_w + 2 * kv_w
    ua = seg(o_u, d_conv)
    ug = seg(o_u + d_conv, d_conv)
    u_ref[...] = ua * jax.nn.sigmoid(ug)
    d = x.shape[1]
    sga_ref[...] = jax.nn.sigmoid(seg(o_u + 2 * d_conv, d))
    sgb_ref[...] = jax.nn.sigmoid(seg(o_u + 2 * d_conv + d, d))


def _premix_call(x, shift, scale, norm1, w_in_bf, qn2, kn2, cos, sin, l, *, rows_per_batch, tm, hd, kv_w):
    n, d = x.shape
    in_w = w_in_bf.shape[-1]
    q_w = d
    d_conv = (in_w - q_w - 2 * kv_w - 2 * d) // 2
    pos_tiles = rows_per_batch // tm
    tok = lambda w: pl.BlockSpec((tm, w), lambda i: (i, 0))
    lay = lambda a: pl.BlockSpec((None,) + a.shape[1:], lambda i: (l,) + (0,) * (a.ndim - 1))
    pos = pl.BlockSpec((tm, LANES), lambda i: (i % pos_tiles, 0))
    kern = functools.partial(_premix_kernel, q_w=q_w, kv_w=kv_w, d_conv=d_conv, hd=hd, qk_scale=hd ** -0.5)
    return pl.pallas_call(
        kern,
        grid=(n // tm,),
        in_specs=[tok(d), _mod_spec(shift, tm, rows_per_batch), _mod_spec(scale, tm, rows_per_batch),
                  lay(norm1), lay(w_in_bf), lay(qn2), lay(kn2), pos, pos],
        out_specs=[tok(q_w), tok(kv_w), tok(kv_w), tok(d_conv), tok(d), tok(d)],
        out_shape=[jax.ShapeDtypeStruct((n, q_w), BF16), jax.ShapeDtypeStruct((n, kv_w), F32),
                   jax.ShapeDtypeStruct((n, kv_w), F32), jax.ShapeDtypeStruct((n, d_conv), F32),
                   jax.ShapeDtypeStruct((n, d), F32), jax.ShapeDtypeStruct((n, d), F32)],
        compiler_params=_params(("arbitrary",)),
        name="premix",
    )(x, shift[0], scale[0], norm1, w_in_bf, qn2, kn2, cos, sin)


def _attn_prompt_kernel(sink_ref, q_ref, kp_ref, kc_ref, vp_ref, vc_ref, o_ref, *, n_heads, group, hd, blocks_per_batch):
    n = pl.program_id(0) % blocks_per_batch
    blk = q_ref.shape[0]
    qi = lax.broadcasted_iota(jnp.int32, (blk, 2 * blk), 0)
    cj = lax.broadcasted_iota(jnp.int32, (blk, 2 * blk), 1)
    valid = ((cj < blk) & (cj > qi) & (n > 0)) | ((cj >= blk) & ((cj - blk) <= qi))
    bias = jnp.where(valid, 0.0, NEG_INF)
    lane = lax.broadcasted_iota(jnp.int32, (blk, LANES), 1)
    lo = lane < hd
    dn = (((1,), (1,)), ((), ()))
    pairs_per_kv = group // 2
    for kh in range(n_heads // group):
        csl = slice((kh // 2) * LANES, (kh // 2 + 1) * LANES)
        keep = lo if kh % 2 == 0 else jnp.logical_not(lo)

        def both_offsets(ref):
            own = jnp.where(keep, ref[:, csl], 0.0)
            swapped = pltpu.roll(own, hd, 1)
            at0, at1 = (own, swapped) if kh % 2 == 0 else (swapped, own)
            return at0.astype(BF16), at1.astype(BF16)

        kp0, kp1 = both_offsets(kp_ref)
        kc0, kc1 = both_offsets(kc_ref)
        vp0, vp1 = both_offsets(vp_ref)
        vc0, vc1 = both_offsets(vc_ref)
        kcat = jnp.concatenate([kp0, kc0, kp1, kc1], axis=0)
        vcat = jnp.concatenate([vp0, vc0, vp1, vc1], axis=0)
        chunk0 = kh * pairs_per_kv
        q2 = jnp.concatenate([q_ref[:, (chunk0 + rp) * LANES:(chunk0 + rp + 1) * LANES] for rp in range(pairs_per_kv)], axis=0)
        s = lax.dot_general(q2, kcat, dn, preferred_element_type=F32)
        p_rows, inv_rows = [], []
        for rp in range(pairs_per_kv):
            p_halves, invs = [], []
            for hp in range(2):
                sink = sink_ref[kh * group + 2 * rp + hp]
                sq = s[rp * blk:(rp + 1) * blk, hp * 2 * blk:(hp + 1) * 2 * blk] + bias
                m = jnp.maximum(jnp.max(sq, axis=-1, keepdims=True), sink)
                p = jnp.exp(sq - m)
                den = jnp.sum(p, axis=-1, keepdims=True) + jnp.exp(sink - m)
                p_halves.append(p.astype(BF16))
                invs.append(1.0 / den)
            p_rows.append(jnp.concatenate(p_halves, axis=1))
            inv_rows.append(jnp.where(lo, invs[0], invs[1]))
        o2 = jnp.dot(jnp.concatenate(p_rows, axis=0), vcat, preferred_element_type=F32)
        for rp in range(pairs_per_kv):
            o_ref[:, (chunk0 + rp) * LANES:(chunk0 + rp + 1) * LANES] = (o2[rp * blk:(rp + 1) * blk] * inv_rows[rp]).astype(BF16)


def _attn_prompt_call(q, k, v, sinks_l, *, rows_per_batch, blk, hd, group):
    n, q_w = q.shape
    kv_w = k.shape[1]
    bpb = rows_per_batch // blk
    cur = lambda i: (i, 0)
    prev = lambda i: (jnp.maximum(i - 1, 0), 0)
    kern = functools.partial(_attn_prompt_kernel, n_heads=q_w // hd, group=group, hd=hd, blocks_per_batch=bpb)
    return pl.pallas_call(
        kern,
        grid=(n // blk,),
        in_specs=[pl.BlockSpec(memory_space=pltpu.SMEM),
                  pl.BlockSpec((blk, q_w), cur),
                  pl.BlockSpec((blk, kv_w), prev), pl.BlockSpec((blk, kv_w), cur),
                  pl.BlockSpec((blk, kv_w), prev), pl.BlockSpec((blk, kv_w), cur)],
        out_specs=pl.BlockSpec((blk, q_w), cur),
        out_shape=jax.ShapeDtypeStruct((n, q_w), BF16),
        compiler_params=_params(("arbitrary",)),
        name="attn_prompt",
    )(sinks_l, q, k, k, v, v)


def _attn_sample_kernel(q_ref, ck_ref, cv_ref, kn_ref, vn_ref, sink_ref, o_ref, ks_ref, vs_ref, *, dec_seq, hd, n_kv):
    bb, win, kv_w = ck_ref.shape
    pad = kn_ref.shape[1]
    ks_ref[:, 0:win, :] = ck_ref[...].astype(BF16)
    vs_ref[:, 0:win, :] = cv_ref[...].astype(BF16)
    ks_ref[:, win:win + pad, :] = kn_ref[...].astype(BF16)
    vs_ref[:, win:win + pad, :] = vn_ref[...].astype(BF16)
    ks_ref[:, win + pad:, :] = jnp.zeros((bb, win - pad, kv_w), BF16)
    vs_ref[:, win + pad:, :] = jnp.zeros((bb, win - pad, kv_w), BF16)

    q = q_ref[...]
    rows = q.shape[1]
    s = jnp.einsum("bqd,bkd->bqk", q, ks_ref[...], preferred_element_type=F32)
    ti = lax.broadcasted_iota(jnp.int32, s.shape, 1) % dec_seq
    kj = lax.broadcasted_iota(jnp.int32, s.shape, 2)
    mask = ((kj < win) & (kj > ti)) | ((kj >= win) & ((kj - win) <= ti))
    s = jnp.where(mask, s, NEG_INF)
    sink = sink_ref[...][:, 0:1]
    m = jnp.maximum(jnp.max(s, axis=-1, keepdims=True), sink)
    p = jnp.exp(s - m)
    den = jnp.sum(p, axis=-1, keepdims=True) + jnp.exp(sink - m)
    o = jnp.einsum("bqk,bkd->bqd", p.astype(BF16), vs_ref[...], preferred_element_type=F32) / den
    rpk = rows // n_kv
    for kh in range(n_kv):
        o_ref[:, kh * rpk:(kh + 1) * rpk, :] = o[:, kh * rpk:(kh + 1) * rpk, kh * hd:(kh + 1) * hd].astype(BF16)


def _attn_sample_call(qblk, cache_k, cache_v, kn, vn, sink_rows, l, *, dec_seq, hd, bb=8):
    b, rows, kv_w = qblk.shape
    win = cache_k.shape[2]
    pad = kn.shape[1]
    n_kv = kv_w // hd
    per_b = lambda a: pl.BlockSpec((bb,) + a.shape[1:], lambda i: (i,) + (0,) * (a.ndim - 1))
    cache = pl.BlockSpec((None, bb, win, kv_w), lambda i: (l, i, 0, 0))
    kern = functools.partial(_attn_sample_kernel, dec_seq=dec_seq, hd=hd, n_kv=n_kv)
    return pl.pallas_call(
        kern,
        grid=(b // bb,),
        in_specs=[per_b(qblk), cache, cache, per_b(kn), per_b(vn),
                  pl.BlockSpec((None,) + sink_rows.shape[1:], lambda i: (l, 0, 0))],
        out_specs=pl.BlockSpec((bb, rows, hd), lambda i: (i, 0, 0)),
        out_shape=jax.ShapeDtypeStruct((b, rows, hd), BF16),
        scratch_shapes=[pltpu.VMEM((bb, 2 * win, kv_w), BF16), pltpu.VMEM((bb, 2 * win, kv_w), BF16)],
        compiler_params=_params(("arbitrary",)),
        name="attn_sample",
    )(qblk, cache_k, cache_v, kn, vn, sink_rows)


def _ln_silu(z, g, b):
    mu = jnp.mean(z, axis=-1, keepdims=True)
    zc = z - mu
    var = jnp.mean(zc * zc, axis=-1, keepdims=True)
    y = zc * lax.rsqrt(var + LN_EPS) * g + b
    return y * jax.nn.sigmoid(y)


def _conv_prompt_kernel(cur_ref, prev_ref, w_ref, b_ref, g_ref, beta_ref, o_ref, ext_ref, sh_ref, z_ref, *, tiles_per_batch, rc):
    tm, d = cur_ref.shape
    cw = w_ref.shape[0]
    first = (pl.program_id(0) % tiles_per_batch) == 0
    ext_ref[CONV_HALO:CONV_HALO + tm, :] = cur_ref[...]
    ext_ref[0:CONV_HALO, :] = jnp.where(first, 0.0, prev_ref[...])
    ls = sh_ref.shape[1]
    for s in range(1, SUBLANES):
        sh_ref[s - 1] = ext_ref[s:s + ls, :]
    base = CONV_HALO - (cw - 1)
    for c in range(d // LANES):
        cs = slice(c * LANES, (c + 1) * LANES)
        for r in range(tm // rc):
            acc = jnp.broadcast_to(b_ref[:, cs], (rc, LANES))
            for j in range(cw):
                a, s = divmod(base + j, SUBLANES)
                r0 = a * SUBLANES + r * rc
                src = ext_ref[r0:r0 + rc, cs] if s == 0 else sh_ref[s - 1, r0:r0 + rc, cs]
                acc = acc + w_ref[j:j + 1, cs] * src
            z_ref[r * rc:(r + 1) * rc, cs] = acc
    o_ref[...] = _ln_silu(z_ref[...], g_ref[...], beta_ref[...]).astype(BF16)


def _conv_prompt_call(u, conv_w, conv_b, ln_g, ln_b, l, *, rows_per_batch, tm, rc=64):
    n, d = u.shape
    halo_blocks = tm // CONV_HALO
    lay = lambda a: pl.BlockSpec((None,) + a.shape[1:], lambda i: (l,) + (0,) * (a.ndim - 1))
    kern = functools.partial(_conv_prompt_kernel, tiles_per_batch=rows_per_batch // tm, rc=rc)
    return pl.pallas_call(
        kern,
        grid=(n // tm,),
        in_specs=[pl.BlockSpec((tm, d), lambda i: (i, 0)),
                  pl.BlockSpec((CONV_HALO, d), lambda i: (jnp.maximum(i * halo_blocks - 1, 0), 0)),
                  lay(conv_w), lay(conv_b), lay(ln_g), lay(ln_b)],
        out_specs=pl.BlockSpec((tm, d), lambda i: (i, 0)),
        out_shape=jax.ShapeDtypeStruct((n, d), BF16),
        scratch_shapes=[pltpu.VMEM((tm + CONV_HALO, d), F32),
                        pltpu.VMEM((SUBLANES - 1, tm + CONV_HALO - SUBLANES, d), F32),
                        pltpu.VMEM((tm, d), F32)],
        compiler_params=_params(("arbitrary",)),
        name="conv_prompt",
    )(u, u, conv_w, conv_b, ln_g, ln_b)


def _conv_sample_kernel(st_ref, u_ref, w_ref, b_ref, g_ref, beta_ref, o_ref, zin_ref, z_ref):
    bb, rows, d = o_ref.shape
    cw = w_ref.shape[0]
    n_state, n_new = st_ref.shape[1], u_ref.shape[1]
    zin_ref[:, 0:n_state, :] = st_ref[...]
    zin_ref[:, n_state:n_state + n_new, :] = u_ref[...]
    zin_ref[:, n_state + n_new:, :] = jnp.zeros((bb, zin_ref.shape[1] - n_state - n_new, d), F32)
    for c in range(d // LANES):
        cs = slice(c * LANES, (c + 1) * LANES)
        acc = jnp.broadcast_to(b_ref[:, cs], (bb, rows, LANES))
        for j in range(cw):
            acc = acc + w_ref[j:j + 1, cs] * zin_ref[:, j:j + rows, cs]
        z_ref[:, :, cs] = acc
    o_ref[...] = _ln_silu(z_ref[...], g_ref[...], beta_ref[...]).astype(BF16)


def _conv_sample_call(state, u3, conv_w, conv_b, ln_g, ln_b, l, *, rows, bb=8):
    b, n_new, d = u3.shape
    n_state = state.shape[2]
    cw = conv_w.shape[1]
    lay = lambda a: pl.BlockSpec((None,) + a.shape[1:], lambda i: (l,) + (0,) * (a.ndim - 1))
    return pl.pallas_call(
        _conv_sample_kernel,
        grid=(b // bb,),
        in_specs=[pl.BlockSpec((None, bb, n_state, d), lambda i: (l, i, 0, 0)),
                  pl.BlockSpec((bb, n_new, d), lambda i: (i, 0, 0)),
                  lay(conv_w), lay(conv_b), lay(ln_g), lay(ln_b)],
        out_specs=pl.BlockSpec((bb, rows, d), lambda i: (i, 0, 0)),
        out_shape=jax.ShapeDtypeStruct((b, rows, d), BF16),
        scratch_shapes=[pltpu.VMEM((bb, rows + cw - 1, d), F32), pltpu.VMEM((bb, rows, d), F32)],
        compiler_params=_params(("arbitrary",)),
        name="conv_sample",
    )(state, u3, conv_w, conv_b, ln_g, ln_b)


def _tail_kernel(*refs, route, n_exp):
    if route:
        (o_ref, c_ref, sga_ref, sgb_ref, x_ref, gate_ref, shift_ref, scale_ref, g2_ref,
         wao_ref, wco_ref, wo_ref, wr_ref, br_ref, xo_ref, h2_ref, rt_ref) = refs
    else:
        (o_ref, c_ref, sga_ref, sgb_ref, x_ref, gate_ref, shift_ref, scale_ref, g2_ref,
         wao_ref, wco_ref, wo_ref, xo_ref, h2_ref) = refs
    a = jnp.dot(o_ref[...], wao_ref[...], preferred_element_type=F32)
    bc = jnp.dot(c_ref[...], wco_ref[...], preferred_element_type=F32)
    m = sga_ref[...] * a + sgb_ref[...] * bc
    y = jnp.dot(m.astype(BF16), wo_ref[...], preferred_element_type=F32)
    x = x_ref[...] + gate_ref[...] * y
    xo_ref[...] = x
    ms = jnp.mean(x * x, axis=-1, keepdims=True)
    h2 = x * lax.rsqrt(ms + RMS_EPS) * g2_ref[...]
    h2 = h2 * (1.0 + scale_ref[...]) + shift_ref[...]
    h2_ref[...] = h2
    if route:
        h_hi = h2.astype(BF16)
        h_lo = (h2 - h_hi.astype(F32)).astype(BF16)
        r = (jnp.dot(h_hi, wr_ref[...], preferred_element_type=F32)
             + jnp.dot(h_lo, wr_ref[...], preferred_element_type=F32))
        logits = r + pltpu.roll(r, LANES - n_exp, 1)
        lane = lax.broadcasted_iota(jnp.int32, logits.shape, 1)
        lanef = lane.astype(F32)
        lg = jnp.where(lane < n_exp, logits + br_ref[...], NEG_INF)
        v1 = jnp.max(lg, axis=-1, keepdims=True)
        i1 = jnp.min(jnp.where(lg == v1, lanef, float(LANES)), axis=-1, keepdims=True)
        lg2 = jnp.where(lanef == i1, NEG_INF, lg)
        v2 = jnp.max(lg2, axis=-1, keepdims=True)
        i2 = jnp.min(jnp.where(lg2 == v2, lanef, float(LANES)), axis=-1, keepdims=True)
        e = jnp.exp(v2 - v1)
        p1 = 1.0 / (1.0 + e)
        p2 = e * p1
        rt_ref[...] = jnp.where(lane == 0, i1, jnp.where(lane == 1, i2, jnp.where(lane == 2, p1, jnp.where(lane == 3, p2, 0.0))))


def _tail_call(o, c, sga, sgb, x, gate, shift, scale, norm2, w_ao, w_co, w_o, l, router, *, n_exp, rows_per_batch, tm):
    n, d = x.shape
    tok = lambda w: pl.BlockSpec((tm, w), lambda i: (i, 0))
    lay = lambda a, ll: pl.BlockSpec((None,) + a.shape[1:], lambda i: (ll,) + (0,) * (a.ndim - 1))
    mod = lambda a: _mod_spec(a, tm, rows_per_batch)
    in_specs = [tok(d), tok(d), tok(d), tok(d), tok(d), mod(gate), mod(shift), mod(scale), lay(norm2, l),
                lay(w_ao, l), lay(w_co, l), lay(w_o, l)]
    args = [o, c, sga, sgb, x, gate[0], shift[0], scale[0], norm2, w_ao, w_co, w_o]
    out_specs = [tok(d), tok(d)]
    out_shape = [jax.ShapeDtypeStruct((n, d), F32), jax.ShapeDtypeStruct((n, d), F32)]
    if router is not None:
        w_r, b_r, li = router
        in_specs += [lay(w_r, li), lay(b_r, li)]
        args += [w_r, b_r]
        out_specs.append(tok(LANES))
        out_shape.append(jax.ShapeDtypeStruct((n, LANES), F32))
    return pl.pallas_call(
        functools.partial(_tail_kernel, route=router is not None, n_exp=n_exp),
        grid=(n // tm,),
        in_specs=in_specs, out_specs=out_specs, out_shape=out_shape,
        compiler_params=_params(("arbitrary",)),
        name="mixer_tail",
    )(*args)


def _swiglu_step(hb_ref, wg_ref, wu_ref, wd_ref, acc_ref):
    hb = hb_ref[...]
    g = jnp.dot(hb, wg_ref[...].astype(BF16), preferred_element_type=F32)
    u = jnp.dot(hb, wu_ref[...].astype(BF16), preferred_element_type=F32)
    a = (g * jax.nn.sigmoid(g)) * u
    acc_ref[...] += jnp.dot(a.astype(BF16), wd_ref[...].astype(BF16), preferred_element_type=F32)


def _ffn_dense_kernel(h_ref, x_ref, gate_ref, wg_ref, wu_ref, wd_ref, o_ref, hb_ref, acc_ref):
    j = pl.program_id(1)

    @pl.when(j == 0)
    def _():
        hb_ref[...] = h_ref[...].astype(BF16)
        acc_ref[...] = jnp.zeros_like(acc_ref)

    _swiglu_step(hb_ref, wg_ref, wu_ref, wd_ref, acc_ref)

    @pl.when(j == pl.num_programs(1) - 1)
    def _():
        o_ref[...] = x_ref[...] + gate_ref[...] * acc_ref[...]


def _ffn_dense_call(h2, x, gate, wg, wu, wd, li, *, rows_per_batch, tm, tf):
    n, d = x.shape
    f = wg.shape[-1]
    tok = pl.BlockSpec((tm, d), lambda i, j: (i, 0))
    return pl.pallas_call(
        _ffn_dense_kernel,
        grid=(n // tm, f // tf),
        in_specs=[tok, tok, _mod_spec(gate, tm, rows_per_batch),
                  pl.BlockSpec((None, d, tf), lambda i, j: (li, 0, j)),
                  pl.BlockSpec((None, d, tf), lambda i, j: (li, 0, j)),
                  pl.BlockSpec((None, tf, d), lambda i, j: (li, j, 0))],
        out_specs=tok,
        out_shape=jax.ShapeDtypeStruct((n, d), F32),
        scratch_shapes=[pltpu.VMEM((tm, d), BF16), pltpu.VMEM((tm, d), F32)],
        compiler_params=_params(("arbitrary", "arbitrary")),
        name="ffn_dense",
    )(h2, x, gate[0], wg, wu, wd)


def _ffn_group_kernel(te_ref, na_ref, nv_ref, src_ref, nsrc_ref, dst_ref, h_hbm, wg_ref, wu_ref, wd_ref, y_hbm,
                      xbuf, hb_ref, acc_ref, ybuf, gsem, ssem):
    m = pl.program_id(0)
    j = pl.program_id(1)
    na = na_ref[0]
    tm = hb_ref.shape[0]
    slot = m % 2

    def gather(idx_ref, sl):
        def body(rb, carry):
            for u in range(DMA_UNROLL):
                r = rb * DMA_UNROLL + u
                pltpu.make_async_copy(h_hbm.at[pl.ds(idx_ref[0, r], 1), :], xbuf.at[sl, pl.ds(r, 1), :],
                                      gsem.at[sl]).start(priority=u % 2)
            return carry
        lax.fori_loop(0, tm // DMA_UNROLL, body, 0)

    def gather_wait(sl):
        def body(rb, carry):
            for u in range(DMA_UNROLL):
                pltpu.make_async_copy(h_hbm.at[pl.ds(0, 1), :], xbuf.at[sl, pl.ds(rb * DMA_UNROLL + u, 1), :],
                                      gsem.at[sl]).wait()
            return carry
        lax.fori_loop(0, tm // DMA_UNROLL, body, 0)

    def scatter_row(r, row):
        return pltpu.make_async_copy(ybuf.at[pl.ds(r, 1), :], y_hbm.at[pl.ds(row, 1), :], ssem)

    def scatter_wait(count):
        def body(r, carry):
            scatter_row(r, 0).wait()
            return carry
        lax.fori_loop(0, count, body, 0)

    @pl.when(m < na)
    def _():
        @pl.when(j == 0)
        def _():
            @pl.when(m == 0)
            def _():
                gather(src_ref, 0)

            @pl.when(m + 1 < na)
            def _():
                gather(nsrc_ref, 1 - slot)

            gather_wait(slot)
            hb_ref[...] = xbuf[slot].astype(BF16)
            acc_ref[...] = jnp.zeros_like(acc_ref)

        _swiglu_step(hb_ref, wg_ref, wu_ref, wd_ref, acc_ref)

        @pl.when(j == pl.num_programs(1) - 1)
        def _():
            @pl.when(m > 0)
            def _():
                scatter_wait(nv_ref[jnp.maximum(m - 1, 0)])

            ybuf[...] = acc_ref[...]

            def body(r, carry):
                scatter_row(r, dst_ref[0, r]).start()
                return carry
            lax.fori_loop(0, nv_ref[m], body, 0)

            @pl.when(m == na - 1)
            def _():
                scatter_wait(nv_ref[m])


def _ffn_group_call(tile_expert, n_active, n_valid, src, dst, h_all, wg, wu, wd, li, *, tm, tf):
    n_tiles = tile_expert.shape[0]
    d = h_all.shape[1]
    f = wg.shape[-1]
    idx = lambda fn: pl.BlockSpec((None, 1, tm), lambda m, j, te, na, nv: (fn(m), 0, 0), memory_space=pltpu.SMEM)
    wspec = lambda shape, fn: pl.BlockSpec((None, None) + shape, lambda m, j, te, na, nv: (li, te[m]) + fn(j))
    grid_spec = pltpu.PrefetchScalarGridSpec(
        num_scalar_prefetch=3,
        grid=(n_tiles, f // tf),
        in_specs=[idx(lambda m: m), idx(lambda m: jnp.minimum(m + 1, n_tiles - 1)), idx(lambda m: m),
                  pl.BlockSpec(memory_space=pl.ANY),
                  wspec((d, tf), lambda j: (0, j)), wspec((d, tf), lambda j: (0, j)), wspec((tf, d), lambda j: (j, 0))],
        out_specs=pl.BlockSpec(memory_space=pl.ANY),
        scratch_shapes=[pltpu.VMEM((2, tm, d), F32), pltpu.VMEM((tm, d), BF16), pltpu.VMEM((tm, d), F32),
                        pltpu.VMEM((tm, d), F32), pltpu.SemaphoreType.DMA((2,)), pltpu.SemaphoreType.DMA(())],
    )
    src3 = src.reshape(n_tiles, 1, tm)
    return pl.pallas_call(
        _ffn_group_kernel,
        grid_spec=grid_spec,
        out_shape=jax.ShapeDtypeStruct((TOP_K * (h_all.shape[0]), d), F32),
        compiler_params=_params(("arbitrary", "arbitrary")),
        name="ffn_grouped",
    )(tile_expert, n_active, n_valid, src3, src3, dst.reshape(n_tiles, 1, tm), h_all, wg, wu, wd)


def _combine_kernel(x_ref, gate_ref, rt_ref, y1_ref, y2_ref, o_ref):
    rt = rt_ref[...]
    o_ref[...] = x_ref[...] + gate_ref[...] * (rt[:, 2:3] * y1_ref[...] + rt[:, 3:4] * y2_ref[...])


def _combine_call(x, gate, route, y_slots, *, tok_off, rows_per_batch, tm):
    n, d = x.shape
    tok = lambda w: pl.BlockSpec((tm, w), lambda i: (i, 0))
    slot = lambda s: pl.BlockSpec((None, tm, d), lambda i: (s, tok_off // tm + i, 0))
    return pl.pallas_call(
        _combine_kernel,
        grid=(n // tm,),
        in_specs=[tok(d), _mod_spec(gate, tm, rows_per_batch), tok(LANES), slot(0), slot(1)],
        out_specs=tok(d),
        out_shape=jax.ShapeDtypeStruct((n, d), F32),
        compiler_params=_params(("arbitrary",)),
        name="moe_combine",
    )(x, gate[0], route, y_slots, y_slots)


def _routing_plan(route_all, n_exp, tm):
    n_tok = route_all.shape[0]
    e = jnp.concatenate([route_all[:, 0], route_all[:, 1]]).astype(jnp.int32)
    onehot = (e[:, None] == jnp.arange(n_exp, dtype=jnp.int32)[None, :]).astype(jnp.int32)
    cnt = jnp.sum(onehot, axis=0)
    rank = jnp.sum((jnp.cumsum(onehot, axis=0) - onehot) * onehot, axis=1)
    padded = ((cnt + tm - 1) // tm) * tm
    ends = jnp.cumsum(padded)
    starts = ends - padded
    pos = starts[e] + rank
    n_rows = TOP_K * n_tok + n_exp * tm
    pair = jnp.arange(TOP_K * n_tok, dtype=jnp.int32)
    dst = jnp.zeros((n_rows,), jnp.int32).at[pos].set(pair)
    src = dst % n_tok
    tile_start = jnp.arange(n_rows // tm, dtype=jnp.int32) * tm
    tile_expert = jnp.minimum(jnp.searchsorted(ends, tile_start, side="right"), n_exp - 1).astype(jnp.int32)
    n_valid = jnp.clip((starts + cnt)[tile_expert] - tile_start, 0, tm).astype(jnp.int32)
    n_active = (ends[-1] // tm).astype(jnp.int32).reshape(1)
    return src, dst, tile_expert, n_valid, n_active


def _rope_tables(pos, hd):
    half = hd // 2
    inv = ROPE_THETA ** (-jnp.arange(0, half, dtype=F32) * 2.0 / hd)
    ang = pos.astype(F32)[:, None] * inv[None, :]
    cos = jnp.cos(ang)
    sin = jnp.sin(ang)
    reps = LANES // hd
    return jnp.tile(jnp.concatenate([cos, cos], axis=1), (1, reps)), jnp.tile(jnp.concatenate([-sin, sin], axis=1), (1, reps))


def kernel(x_prompt, x_sample, cache_k_win, cache_v_win, state_conv, c_prompt, c_sample, w_ada, b_ada, norm1, norm2, w_in, q_norm, k_norm, sinks, w_attn_out, conv_w, conv_b, conv_ln_g, conv_ln_b, w_conv_out, w_o, w_ff_gate, w_ff_up, w_ff_down, w_router, b_router, w_e_gate, w_e_up, w_e_down):
    bsz, seq, d = x_prompt.shape
    dec_b, dec_s, _ = x_sample.shape
    depth = w_in.shape[0]
    hd = q_norm.shape[1]
    n_heads = sinks.shape[1]
    win, n_kv = cache_k_win.shape[2], cache_k_win.shape[3]
    kv_w = n_kv * hd
    group = n_heads // n_kv
    cw = conv_w.shape[1]
    n_exp = w_router.shape[-1]
    np_tok, ns_tok = bsz * seq, dec_b * dec_s
    n_tok = np_tok + ns_tok

    c_all = jnp.concatenate([c_prompt, c_sample], axis=0)
    bp = -(-c_all.shape[0] // 8) * 8
    mods = _ada_call(jnp.pad(c_all, ((0, bp - c_all.shape[0]), (0, 0))), w_ada, b_ada)
    mods = mods.reshape(depth, bp, 6, d)
    mods_p = mods[:, :bsz].transpose(0, 2, 1, 3)[:, :, :, None, :]
    mods_s = jnp.repeat(mods[:, bsz:bsz + dec_b], dec_s, axis=1).transpose(0, 2, 1, 3)[:, :, None]
    mod_p = [[(mods_p, l, i) for i in range(6)] for l in range(depth)]
    mod_s = [[(mods_s, l, i) for i in range(6)] for l in range(depth)]

    w_in_bf, w_ao_bf, w_co_bf, w_o_bf = (w.astype(BF16) for w in (w_in, w_attn_out, w_conv_out, w_o))
    w_fg_bf, w_fu_bf, w_fd_bf = (w.astype(BF16) for w in (w_ff_gate, w_ff_up, w_ff_down))
    w_r_hi = w_router.astype(BF16)
    w_r_lo = (w_router - w_r_hi.astype(F32)).astype(BF16)
    w_r_pad = jnp.pad(jnp.concatenate([w_r_hi, w_r_lo], axis=-1), ((0, 0), (0, 0), (0, LANES - 2 * n_exp)))
    b_r3 = jnp.pad(b_router, ((0, 0), (0, LANES - n_exp))).reshape(-1, 1, LANES)

    reps = LANES // hd
    qn2 = jnp.tile(q_norm, (1, reps)).reshape(depth, 1, LANES)
    kn2 = jnp.tile(k_norm, (1, reps)).reshape(depth, 1, LANES)
    cos_p, sin_p = _rope_tables(jnp.arange(seq, dtype=jnp.int32), hd)
    cos_s, sin_s = _rope_tables(jnp.tile(PAST_LEN + jnp.arange(dec_s, dtype=jnp.int32), dec_b), hd)
    r3 = lambda a: a.reshape(depth, 1, a.shape[-1])
    norm1_3, norm2_3, conv_b3, ln_g3, ln_b3 = r3(norm1), r3(norm2), r3(conv_b), r3(conv_ln_g), r3(conv_ln_b)
    cache_k = cache_k_win.reshape(depth, dec_b, win, kv_w)
    cache_v = cache_v_win.reshape(depth, dec_b, win, kv_w)
    head_kv = (jnp.arange(n_heads) // group)
    blockdiag = (head_kv[:, None] == jnp.arange(n_kv)[None, :]).astype(BF16)
    new_pad = 16
    sink_rows = jnp.broadcast_to(jnp.repeat(sinks, dec_s, axis=1)[:, :, None], (depth, n_heads * dec_s, LANES))

    xp = x_prompt.reshape(np_tok, d)
    xs = x_sample.reshape(ns_tok, d)
    outs = {k: [] for k in ("kp", "vp", "cp", "ks", "vs", "cs")}
    tm_moe = 512

    for l in range(depth):
        mp, ms = mod_p[l], mod_s[l]
        q, k, v, u, sga, sgb = _premix_call(xp, mp[0], mp[1], norm1_3, w_in_bf, qn2, kn2, cos_p, sin_p, l,
                                            rows_per_batch=seq, tm=256, hd=hd, kv_w=kv_w)
        o = _attn_prompt_call(q, k, v, sinks[l], rows_per_batch=seq, blk=win, hd=hd, group=group)
        cv = _conv_prompt_call(u, conv_w, conv_b3, ln_g3, ln_b3, l, rows_per_batch=seq, tm=256)
        outs["kp"].append(k.reshape(bsz, seq, kv_w)[:, seq - win:].reshape(bsz, win, n_kv, hd))
        outs["vp"].append(v.reshape(bsz, seq, kv_w)[:, seq - win:].reshape(bsz, win, n_kv, hd))
        outs["cp"].append(u.reshape(bsz, seq, d)[:, seq - (cw - 1):])
        moe_layer = l % 2 == 1
        router = (w_r_pad, b_r3, l // 2) if moe_layer else None
        res_p = _tail_call(o, cv, sga, sgb, xp, mp[2], mp[3], mp[4], norm2_3, w_ao_bf, w_co_bf, w_o_bf, l, router,
                           n_exp=n_exp, rows_per_batch=seq, tm=256)

        q, k, v, u, sga, sgb = _premix_call(xs, ms[0], ms[1], norm1_3, w_in_bf, qn2, kn2, cos_s, sin_s, l,
                                            rows_per_batch=ns_tok, tm=256, hd=hd, kv_w=kv_w)
        q4 = q.reshape(dec_b, dec_s, n_heads, hd).transpose(0, 2, 1, 3)
        qblk = (q4[:, :, :, None, :] * blockdiag[None, :, None, :, None]).reshape(dec_b, n_heads * dec_s, kv_w)
        k3, v3 = k.reshape(dec_b, dec_s, kv_w), v.reshape(dec_b, dec_s, kv_w)
        padn = lambda a: jnp.pad(a, ((0, 0), (0, new_pad - dec_s), (0, 0)))
        o = _attn_sample_call(qblk, cache_k, cache_v, padn(k3), padn(v3), sink_rows, l, dec_seq=dec_s, hd=hd)
        o = o.reshape(dec_b, n_heads, dec_s, hd).transpose(0, 2, 1, 3).reshape(ns_tok, n_heads * hd)
        u3 = u.reshape(dec_b, dec_s, d)
        cv = _conv_sample_call(state_conv, u3, conv_w, conv_b3, ln_g3, ln_b3, l, rows=SUBLANES)[:, :dec_s].reshape(ns_tok, d)
        outs["ks"].append(k3)
        outs["vs"].append(v3)
        outs["cs"].append(u3)
        res_s = _tail_call(o, cv, sga, sgb, xs, ms[2], ms[3], ms[4], norm2_3, w_ao_bf, w_co_bf, w_o_bf, l, router,
                           n_exp=n_exp, rows_per_batch=ns_tok, tm=256)

        if not moe_layer:
            xp, h2p = res_p
            xs, h2s = res_s
            xp = _ffn_dense_call(h2p, xp, mp[5], w_fg_bf, w_fu_bf, w_fd_bf, l // 2, rows_per_batch=seq, tm=512, tf=1408)
            xs = _ffn_dense_call(h2s, xs, ms[5], w_fg_bf, w_fu_bf, w_fd_bf, l // 2, rows_per_batch=ns_tok, tm=512, tf=1408)
        else:
            xp, h2p, rtp = res_p
            xs, h2s, rts = res_s
            h_all = jnp.concatenate([h2p, h2s], axis=0)
            rt_all = jnp.concatenate([rtp, rts], axis=0)
            src, dst, tile_expert, n_valid, n_active = _routing_plan(rt_all, n_exp, tm_moe)
            y = _ffn_group_call(tile_expert, n_active, n_valid, src, dst, h_all, w_e_gate, w_e_up, w_e_down, l // 2,
                                tm=tm_moe, tf=896).reshape(TOP_K, n_tok, d)
            xp = _combine_call(xp, mp[5], rtp, y, tok_off=0, rows_per_batch=seq, tm=256)
            xs = _combine_call(xs, ms[5], rts, y, tok_off=np_tok, rows_per_batch=ns_tok, tm=256)

    new_kv = lambda rows: jnp.stack(rows).reshape(depth, dec_b, dec_s, n_kv, hd)
    k_win_s = jnp.concatenate([cache_k_win[:, :, dec_s:], new_kv(outs["ks"])], axis=2)
    v_win_s = jnp.concatenate([cache_v_win[:, :, dec_s:], new_kv(outs["vs"])], axis=2)
    conv_s = jnp.concatenate([state_conv[:, :, dec_s:], jnp.stack(outs["cs"])], axis=2)
    return (xp.reshape(bsz, seq, d), xs.reshape(dec_b, dec_s, d),
            jnp.stack(outs["kp"]), jnp.stack(outs["vp"]), jnp.stack(outs["cp"]),
            k_win_s, v_win_s, conv_s)
```

```python
import functools

import jax
import jax.numpy as jnp
from jax import lax
from jax.experimental import pallas as pl
from jax.experimental.pallas import tpu as pltpu

F32 = jnp.float32
BF16 = jnp.bfloat16

PAST_LEN = 8192
ROPE_THETA = 10000.0
RMS_EPS = 1e-6
LN_EPS = 1e-5
TOP_K = 2
LANES = 128
SUBLANES = 8
CONV_HALO = 32
VMEM_LIMIT = 56 * 1024 * 1024
DMA_UNROLL = 8

NEG_INF = float("-inf")


def _params(sem, vmem=VMEM_LIMIT):
    return pltpu.CompilerParams(dimension_semantics=sem, vmem_limit_bytes=vmem)


def _mod_spec(mod, tm, rows_per_batch):
    arr, l, which = mod
    d = arr.shape[-1]
    if arr.shape[3] == 1:
        return pl.BlockSpec((None, None, None, 1, d), lambda i, *_: (l, which, (i * tm) // rows_per_batch, 0, 0))
    return pl.BlockSpec((None, None, None, tm, d), lambda i, *_: (l, which, 0, i, 0))


def _ada_kernel(c_ref, w_ref, b_ref, o_ref):
    c = c_ref[...]
    a = (c * jax.nn.sigmoid(c)).astype(BF16)
    o_ref[...] = jnp.dot(a, w_ref[...].astype(BF16), preferred_element_type=F32) + b_ref[...]


def _ada_call(c_all, w_ada, b_ada, tn=1536):
    depth, d, n = w_ada.shape
    bp = c_all.shape[0]
    return pl.pallas_call(
        _ada_kernel,
        grid=(depth, n // tn),
        in_specs=[
            pl.BlockSpec((bp, d), lambda l, j: (0, 0)),
            pl.BlockSpec((None, d, tn), lambda l, j: (l, 0, j)),
            pl.BlockSpec((None, 1, tn), lambda l, j: (l, 0, j)),
        ],
        out_specs=pl.BlockSpec((None, bp, tn), lambda l, j: (l, 0, j)),
        out_shape=jax.ShapeDtypeStruct((depth, bp, n), F32),
        compiler_params=_params(("arbitrary", "arbitrary")),
        name="ada_mod",
    )(c_all, w_ada, b_ada.reshape(depth, 1, n))


def _head_norm_rope(xc, nw, cos, sin, lo, first_half, hd):
    t = xc * xc
    s_lo = jnp.sum(jnp.where(lo, t, 0.0), axis=-1, keepdims=True)
    s_hi = jnp.sum(jnp.where(lo, 0.0, t), axis=-1, keepdims=True)
    inv = jnp.where(lo, lax.rsqrt(s_lo / hd + RMS_EPS), lax.rsqrt(s_hi / hd + RMS_EPS))
    y = xc * inv * nw
    rot = jnp.where(first_half, pltpu.roll(y, LANES - hd // 2, 1), pltpu.roll(y, hd // 2, 1))
    return y * cos + rot * sin


def _premix_kernel(x_ref, shift_ref, scale_ref, g_ref, w_ref, qn_ref, kn_ref, cos_ref, sin_ref,
                   q_ref, k_ref, v_ref, u_ref, sga_ref, sgb_ref, *, q_w, kv_w, d_conv, hd, qk_scale):
    x = x_ref[...]
    ms = jnp.mean(x * x, axis=-1, keepdims=True)
    h = x * lax.rsqrt(ms + RMS_EPS) * g_ref[...]
    h = h * (1.0 + scale_ref[...]) + shift_ref[...]
    hb = h.astype(BF16)

    lane = lax.broadcasted_iota(jnp.int32, (x.shape[0], LANES), 1)
    lo = lane < hd
    first_half = (lane % hd) < (hd // 2)
    cos = cos_ref[...]
    sin = sin_ref[...]

    def seg(start, width):
        return jnp.dot(hb, w_ref[:, start:start + width], preferred_element_type=F32)

    qf = seg(0, q_w)
    for c in range(q_w // LANES):
        out = _head_norm_rope(qf[:, c * LANES:(c + 1) * LANES], qn_ref[...], cos, sin, lo, first_half, hd)
        q_ref[:, c * LANES:(c + 1) * LANES] = (out * qk_scale).astype(BF16)
    kf = seg(q_w, kv_w)
    for c in range(kv_w // LANES):
        k_ref[:, c * LANES:(c + 1) * LANES] = _head_norm_rope(kf[:, c * LANES:(c + 1) * LANES], kn_ref[...], cos, sin, lo, first_half, hd)
    v_ref[...] = seg(q_w + kv_w, kv_w)
    o_u = q_w + 2 * kv_w
    ua = seg(o_u, d_conv)
    ug = seg(o_u + d_conv, d_conv)
    u_ref[...] = ua * jax.nn.sigmoid(ug)
    d = x.shape[1]
    sga_ref[...] = jax.nn.sigmoid(seg(o_u + 2 * d_conv, d))
    sgb_ref[...] = jax.nn.sigmoid(seg(o_u + 2 * d_conv + d, d))


def _premix_call(x, shift, scale, norm1, w_in_bf, qn2, kn2, cos, sin, l, *, rows_per_batch, tm, hd, kv_w):
    n, d = x.shape
    in_w = w_in_bf.shape[-1]
    q_w = d
    d_conv = (in_w - q_w - 2 * kv_w - 2 * d) // 2
    pos_tiles = rows_per_batch // tm
    tok = lambda w: pl.BlockSpec((tm, w), lambda i: (i, 0))
    lay = lambda a: pl.BlockSpec((None,) + a.shape[1:], lambda i: (l,) + (0,) * (a.ndim - 1))
    pos = pl.BlockSpec((tm, LANES), lambda i: (i % pos_tiles, 0))
    kern = functools.partial(_premix_kernel, q_w=q_w, kv_w=kv_w, d_conv=d_conv, hd=hd, qk_scale=hd ** -0.5)
    return pl.pallas_call(
        kern,
        grid=(n // tm,),
        in_specs=[tok(d), _mod_spec(shift, tm, rows_per_batch), _mod_spec(scale, tm, rows_per_batch),
                  lay(norm1), lay(w_in_bf), lay(qn2), lay(kn2), pos, pos],
        out_specs=[tok(q_w), tok(kv_w), tok(kv_w), tok(d_conv), tok(d), tok(d)],
        out_shape=[jax.ShapeDtypeStruct((n, q_w), BF16), jax.ShapeDtypeStruct((n, kv_w), F32),
                   jax.ShapeDtypeStruct((n, kv_w), F32), jax.ShapeDtypeStruct((n, d_conv), F32),
                   jax.ShapeDtypeStruct((n, d), F32), jax.ShapeDtypeStruct((n, d), F32)],
        compiler_params=_params(("arbitrary",)),
        name="premix",
    )(x, shift[0], scale[0], norm1, w_in_bf, qn2, kn2, cos, sin)


def _attn_prompt_kernel(sink_ref, q_ref, kp_ref, kc_ref, vp_ref, vc_ref, o_ref, *, n_heads, group, hd, blocks_per_batch):
    n = pl.program_id(0) % blocks_per_batch
    blk = q_ref.shape[0]
    qi = lax.broadcasted_iota(jnp.int32, (blk, 2 * blk), 0)
    cj = lax.broadcasted_iota(jnp.int32, (blk, 2 * blk), 1)
    valid = ((cj < blk) & (cj > qi) & (n > 0)) | ((cj >= blk) & ((cj - blk) <= qi))
    bias = jnp.where(valid, 0.0, NEG_INF)
    lane = lax.broadcasted_iota(jnp.int32, (blk, LANES), 1)
    lo = lane < hd
    dn = (((1,), (1,)), ((), ()))
    pairs_per_kv = group // 2
    for kh in range(n_heads // group):
        csl = slice((kh // 2) * LANES, (kh // 2 + 1) * LANES)
        keep = lo if kh % 2 == 0 else jnp.logical_not(lo)

        def both_offsets(ref):
            own = jnp.where(keep, ref[:, csl], 0.0)
            swapped = pltpu.roll(own, hd, 1)
            at0, at1 = (own, swapped) if kh % 2 == 0 else (swapped, own)
            return at0.astype(BF16), at1.astype(BF16)

        kp0, kp1 = both_offsets(kp_ref)
        kc0, kc1 = both_offsets(kc_ref)
        vp0, vp1 = both_offsets(vp_ref)
        vc0, vc1 = both_offsets(vc_ref)
        kcat = jnp.concatenate([kp0, kc0, kp1, kc1], axis=0)
        vcat = jnp.concatenate([vp0, vc0, vp1, vc1], axis=0)
        chunk0 = kh * pairs_per_kv
        q2 = jnp.concatenate([q_ref[:, (chunk0 + rp) * LANES:(chunk0 + rp + 1) * LANES] for rp in range(pairs_per_kv)], axis=0)
        s = lax.dot_general(q2, kcat, dn, preferred_element_type=F32)
        p_rows, inv_rows = [], []
        for rp in range(pairs_per_kv):
            p_halves, invs = [], []
            for hp in range(2):
                sink = sink_ref[kh * group + 2 * rp + hp]
                sq = s[rp * blk:(rp + 1) * blk, hp * 2 * blk:(hp + 1) * 2 * blk] + bias
                m = jnp.maximum(jnp.max(sq, axis=-1, keepdims=True), sink)
                p = jnp.exp(sq - m)
                den = jnp.sum(p, axis=-1, keepdims=True) + jnp.exp(sink - m)
                p_halves.append(p.astype(BF16))
                invs.append(1.0 / den)
            p_rows.append(jnp.concatenate(p_halves, axis=1))
            inv_rows.append(jnp.where(lo, invs[0], invs[1]))
        o2 = jnp.dot(jnp.concatenate(p_rows, axis=0), vcat, preferred_element_type=F32)
        for rp in range(pairs_per_kv):
            o_ref[:, (chunk0 + rp) * LANES:(chunk0 + rp + 1) * LANES] = (o2[rp * blk:(rp + 1) * blk] * inv_rows[rp]).astype(BF16)


def _attn_prompt_call(q, k, v, sinks_l, *, rows_per_batch, blk, hd, group):
    n, q_w = q.shape
    kv_w = k.shape[1]
    bpb = rows_per_batch // blk
    cur = lambda i: (i, 0)
    prev = lambda i: (jnp.maximum(i - 1, 0), 0)
    kern = functools.partial(_attn_prompt_kernel, n_heads=q_w // hd, group=group, hd=hd, blocks_per_batch=bpb)
    return pl.pallas_call(
        kern,
        grid=(n // blk,),
        in_specs=[pl.BlockSpec(memory_space=pltpu.SMEM),
                  pl.BlockSpec((blk, q_w), cur),
                  pl.BlockSpec((blk, kv_w), prev), pl.BlockSpec((blk, kv_w), cur),
                  pl.BlockSpec((blk, kv_w), prev), pl.BlockSpec((blk, kv_w), cur)],
        out_specs=pl.BlockSpec((blk, q_w), cur),
        out_shape=jax.ShapeDtypeStruct((n, q_w), BF16),
        compiler_params=_params(("arbitrary",)),
        name="attn_prompt",
    )(sinks_l, q, k, k, v, v)


def _attn_sample_kernel(q_ref, ck_ref, cv_ref, kn_ref, vn_ref, sink_ref, o_ref, ks_ref, vs_ref, *, dec_seq, hd, n_kv):
    bb, win, kv_w = ck_ref.shape
    pad = kn_ref.shape[1]
    ks_ref[:, 0:win, :] = ck_ref[...].astype(BF16)
    vs_ref[:, 0:win, :] = cv_ref[...].astype(BF16)
    ks_ref[:, win:win + pad, :] = kn_ref[...].astype(BF16)
    vs_ref[:, win:win + pad, :] = vn_ref[...].astype(BF16)
    ks_ref[:, win + pad:, :] = jnp.zeros((bb, win - pad, kv_w), BF16)
    vs_ref[:, win + pad:, :] = jnp.zeros((bb, win - pad, kv_w), BF16)

    q = q_ref[...]
    rows = q.shape[1]
    s = jnp.einsum("bqd,bkd->bqk", q, ks_ref[...], preferred_element_type=F32)
    ti = lax.broadcasted_iota(jnp.int32, s.shape, 1) % dec_seq
    kj = lax.broadcasted_iota(jnp.int32, s.shape, 2)
    mask = ((kj < win) & (kj > ti)) | ((kj >= win) & ((kj - win) <= ti))
    s = jnp.where(mask, s, NEG_INF)
    sink = sink_ref[...][:, 0:1]
    m = jnp.maximum(jnp.max(s, axis=-1, keepdims=True), sink)
    p = jnp.exp(s - m)
    den = jnp.sum(p, axis=-1, keepdims=True) + jnp.exp(sink - m)
    o = jnp.einsum("bqk,bkd->bqd", p.astype(BF16), vs_ref[...], preferred_element_type=F32) / den
    rpk = rows // n_kv
    for kh in range(n_kv):
        o_ref[:, kh * rpk:(kh + 1) * rpk, :] = o[:, kh * rpk:(kh + 1) * rpk, kh * hd:(kh + 1) * hd].astype(BF16)


def _attn_sample_call(qblk, cache_k, cache_v, kn, vn, sink_rows, l, *, dec_seq, hd, bb=8):
    b, rows, kv_w = qblk.shape
    win = cache_k.shape[2]
    pad = kn.shape[1]
    n_kv = kv_w // hd
    per_b = lambda a: pl.BlockSpec((bb,) + a.shape[1:], lambda i: (i,) + (0,) * (a.ndim - 1))
    cache = pl.BlockSpec((None, bb, win, kv_w), lambda i: (l, i, 0, 0))
    kern = functools.partial(_attn_sample_kernel, dec_seq=dec_seq, hd=hd, n_kv=n_kv)
    return pl.pallas_call(
        kern,
        grid=(b // bb,),
        in_specs=[per_b(qblk), cache, cache, per_b(kn), per_b(vn),
                  pl.BlockSpec((None,) + sink_rows.shape[1:], lambda i: (l, 0, 0))],
        out_specs=pl.BlockSpec((bb, rows, hd), lambda i: (i, 0, 0)),
        out_shape=jax.ShapeDtypeStruct((b, rows, hd), BF16),
        scratch_shapes=[pltpu.VMEM((bb, 2 * win, kv_w), BF16), pltpu.VMEM((bb, 2 * win, kv_w), BF16)],
        compiler_params=_params(("arbitrary",)),
        name="attn_sample",
    )(qblk, cache_k, cache_v, kn, vn, sink_rows)


def _ln_silu(z, g, b):
    mu = jnp.mean(z, axis=-1, keepdims=True)
    zc = z - mu
    var = jnp.mean(zc * zc, axis=-1, keepdims=True)
    y = zc * lax.rsqrt(var + LN_EPS) * g + b
    return y * jax.nn.sigmoid(y)


def _conv_prompt_kernel(cur_ref, prev_ref, w_ref, b_ref, g_ref, beta_ref, o_ref, ext_ref, sh_ref, z_ref, *, tiles_per_batch, rc):
    tm, d = cur_ref.shape
    cw = w_ref.shape[0]
    first = (pl.program_id(0) % tiles_per_batch) == 0
    ext_ref[CONV_HALO:CONV_HALO + tm, :] = cur_ref[...]
    ext_ref[0:CONV_HALO, :] = jnp.where(first, 0.0, prev_ref[...])
    ls = sh_ref.shape[1]
    for s in range(1, SUBLANES):
        sh_ref[s - 1] = ext_ref[s:s + ls, :]
    base = CONV_HALO - (cw - 1)
    for c in range(d // LANES):
        cs = slice(c * LANES, (c + 1) * LANES)
        for r in range(tm // rc):
            acc = jnp.broadcast_to(b_ref[:, cs], (rc, LANES))
            for j in range(cw):
                a, s = divmod(base + j, SUBLANES)
                r0 = a * SUBLANES + r * rc
                src = ext_ref[r0:r0 + rc, cs] if s == 0 else sh_ref[s - 1, r0:r0 + rc, cs]
                acc = acc + w_ref[j:j + 1, cs] * src
            z_ref[r * rc:(r + 1) * rc, cs] = acc
    o_ref[...] = _ln_silu(z_ref[...], g_ref[...], beta_ref[...]).astype(BF16)


def _conv_prompt_call(u, conv_w, conv_b, ln_g, ln_b, l, *, rows_per_batch, tm, rc=64):
    n, d = u.shape
    halo_blocks = tm // CONV_HALO
    lay = lambda a: pl.BlockSpec((None,) + a.shape[1:], lambda i: (l,) + (0,) * (a.ndim - 1))
    kern = functools.partial(_conv_prompt_kernel, tiles_per_batch=rows_per_batch // tm, rc=rc)
    return pl.pallas_call(
        kern,
        grid=(n // tm,),
        in_specs=[pl.BlockSpec((tm, d), lambda i: (i, 0)),
                  pl.BlockSpec((CONV_HALO, d), lambda i: (jnp.maximum(i * halo_blocks - 1, 0), 0)),
                  lay(conv_w), lay(conv_b), lay(ln_g), lay(ln_b)],
        out_specs=pl.BlockSpec((tm, d), lambda i: (i, 0)),
        out_shape=jax.ShapeDtypeStruct((n, d), BF16),
        scratch_shapes=[pltpu.VMEM((tm + CONV_HALO, d), F32),
                        pltpu.VMEM((SUBLANES - 1, tm + CONV_HALO - SUBLANES, d), F32),
                        pltpu.VMEM((tm, d), F32)],
        compiler_params=_params(("arbitrary",)),
        name="conv_prompt",
    )(u, u, conv_w, conv_b, ln_g, ln_b)


def _conv_sample_kernel(st_ref, u_ref, w_ref, b_ref, g_ref, beta_ref, o_ref, zin_ref, z_ref):
    bb, rows, d = o_ref.shape
    cw = w_ref.shape[0]
    n_state, n_new = st_ref.shape[1], u_ref.shape[1]
    zin_ref[:, 0:n_state, :] = st_ref[...]
    zin_ref[:, n_state:n_state + n_new, :] = u_ref[...]
    zin_ref[:, n_state + n_new:, :] = jnp.zeros((bb, zin_ref.shape[1] - n_state - n_new, d), F32)
    for c in range(d // LANES):
        cs = slice(c * LANES, (c + 1) * LANES)
        acc = jnp.broadcast_to(b_ref[:, cs], (bb, rows, LANES))
        for j in range(cw):
            acc = acc + w_ref[j:j + 1, cs] * zin_ref[:, j:j + rows, cs]
        z_ref[:, :, cs] = acc
    o_ref[...] = _ln_silu(z_ref[...], g_ref[...], beta_ref[...]).astype(BF16)


def _conv_sample_call(state, u3, conv_w, conv_b, ln_g, ln_b, l, *, rows, bb=8):
    b, n_new, d = u3.shape
    n_state = state.shape[2]
    cw = conv_w.shape[1]
    lay = lambda a: pl.BlockSpec((None,) + a.shape[1:], lambda i: (l,) + (0,) * (a.ndim - 1))
    return pl.pallas_call(
        _conv_sample_kernel,
        grid=(b // bb,),
        in_specs=[pl.BlockSpec((None, bb, n_state, d), lambda i: (l, i, 0, 0)),
                  pl.BlockSpec((bb, n_new, d), lambda i: (i, 0, 0)),
                  lay(conv_w), lay(conv_b), lay(ln_g), lay(ln_b)],
        out_specs=pl.BlockSpec((bb, rows, d), lambda i: (i, 0, 0)),
        out_shape=jax.ShapeDtypeStruct((b, rows, d), BF16),
        scratch_shapes=[pltpu.VMEM((bb, rows + cw - 1, d), F32), pltpu.VMEM((bb, rows, d), F32)],
        compiler_params=_params(("arbitrary",)),
        name="conv_sample",
    )(state, u3, conv_w, conv_b, ln_g, ln_b)


def _tail_kernel(*refs, route, n_exp):
    if route:
        (o_ref, c_ref, sga_ref, sgb_ref, x_ref, gate_ref, shift_ref, scale_ref, g2_ref,
         wao_ref, wco_ref, wo_ref, wr_ref, br_ref, xo_ref, h2_ref, rt_ref) = refs
    else:
        (o_ref, c_ref, sga_ref, sgb_ref, x_ref, gate_ref, shift_ref, scale_ref, g2_ref,
         wao_ref, wco_ref, wo_ref, xo_ref, h2_ref) = refs
    a = jnp.dot(o_ref[...], wao_ref[...], preferred_element_type=F32)
    bc = jnp.dot(c_ref[...], wco_ref[...], preferred_element_type=F32)
    m = sga_ref[...] * a + sgb_ref[...] * bc
    y = jnp.dot(m.astype(BF16), wo_ref[...], preferred_element_type=F32)
    x = x_ref[...] + gate_ref[...] * y
    xo_ref[...] = x
    ms = jnp.mean(x * x, axis=-1, keepdims=True)
    h2 = x * lax.rsqrt(ms + RMS_EPS) * g2_ref[...]
    h2 = h2 * (1.0 + scale_ref[...]) + shift_ref[...]
    h2_ref[...] = h2
    if route:
        h_hi = h2.astype(BF16)
        h_lo = (h2 - h_hi.astype(F32)).astype(BF16)
        r = (jnp.dot(h_hi, wr_ref[...], preferred_element_type=F32)
             + jnp.dot(h_lo, wr_ref[...], preferred_element_type=F32))
        logits = r + pltpu.roll(r, LANES - n_exp, 1)
        lane = lax.broadcasted_iota(jnp.int32, logits.shape, 1)
        lanef = lane.astype(F32)
        lg = jnp.where(lane < n_exp, logits + br_ref[...], NEG_INF)
        v1 = jnp.max(lg, axis=-1, keepdims=True)
        i1 = jnp.min(jnp.where(lg == v1, lanef, float(LANES)), axis=-1, keepdims=True)
        lg2 = jnp.where(lanef == i1, NEG_INF, lg)
        v2 = jnp.max(lg2, axis=-1, keepdims=True)
        i2 = jnp.min(jnp.where(lg2 == v2, lanef, float(LANES)), axis=-1, keepdims=True)
        e = jnp.exp(v2 - v1)
        p1 = 1.0 / (1.0 + e)
        p2 = e * p1
        rt_ref[...] = jnp.where(lane == 0, i1, jnp.where(lane == 1, i2, jnp.where(lane == 2, p1, jnp.where(lane == 3, p2, 0.0))))


def _tail_call(o, c, sga, sgb, x, gate, shift, scale, norm2, w_ao, w_co, w_o, l, router, *, n_exp, rows_per_batch, tm):
    n, d = x.shape
    tok = lambda w: pl.BlockSpec((tm, w), lambda i: (i, 0))
    lay = lambda a, ll: pl.BlockSpec((None,) + a.shape[1:], lambda i: (ll,) + (0,) * (a.ndim - 1))
    mod = lambda a: _mod_spec(a, tm, rows_per_batch)
    in_specs = [tok(d), tok(d), tok(d), tok(d), tok(d), mod(gate), mod(shift), mod(scale), lay(norm2, l),
                lay(w_ao, l), lay(w_co, l), lay(w_o, l)]
    args = [o, c, sga, sgb, x, gate[0], shift[0], scale[0], norm2, w_ao, w_co, w_o]
    out_specs = [tok(d), tok(d)]
    out_shape = [jax.ShapeDtypeStruct((n, d), F32), jax.ShapeDtypeStruct((n, d), F32)]
    if router is not None:
        w_r, b_r, li = router
        in_specs += [lay(w_r, li), lay(b_r, li)]
        args += [w_r, b_r]
        out_specs.append(tok(LANES))
        out_shape.append(jax.ShapeDtypeStruct((n, LANES), F32))
    return pl.pallas_call(
        functools.partial(_tail_kernel, route=router is not None, n_exp=n_exp),
        grid=(n // tm,),
        in_specs=in_specs, out_specs=out_specs, out_shape=out_shape,
        compiler_params=_params(("arbitrary",)),
        name="mixer_tail",
    )(*args)


def _swiglu_step(hb_ref, wg_ref, wu_ref, wd_ref, acc_ref):
    hb = hb_ref[...]
    g = jnp.dot(hb, wg_ref[...].astype(BF16), preferred_element_type=F32)
    u = jnp.dot(hb, wu_ref[...].astype(BF16), preferred_element_type=F32)
    a = (g * jax.nn.sigmoid(g)) * u
    acc_ref[...] += jnp.dot(a.astype(BF16), wd_ref[...].astype(BF16), preferred_element_type=F32)


def _ffn_dense_kernel(h_ref, x_ref, gate_ref, wg_ref, wu_ref, wd_ref, o_ref, hb_ref, acc_ref):
    j = pl.program_id(1)

    @pl.when(j == 0)
    def _():
        hb_ref[...] = h_ref[...].astype(BF16)
        acc_ref[...] = jnp.zeros_like(acc_ref)

    _swiglu_step(hb_ref, wg_ref, wu_ref, wd_ref, acc_ref)

    @pl.when(j == pl.num_programs(1) - 1)
    def _():
        o_ref[...] = x_ref[...] + gate_ref[...] * acc_ref[...]


def _ffn_dense_call(h2, x, gate, wg, wu, wd, li, *, rows_per_batch, tm, tf):
    n, d = x.shape
    f = wg.shape[-1]
    tok = pl.BlockSpec((tm, d), lambda i, j: (i, 0))
    return pl.pallas_call(
        _ffn_dense_kernel,
        grid=(n // tm, f // tf),
        in_specs=[tok, tok, _mod_spec(gate, tm, rows_per_batch),
                  pl.BlockSpec((None, d, tf), lambda i, j: (li, 0, j)),
                  pl.BlockSpec((None, d, tf), lambda i, j: (li, 0, j)),
                  pl.BlockSpec((None, tf, d), lambda i, j: (li, j, 0))],
        out_specs=tok,
        out_shape=jax.ShapeDtypeStruct((n, d), F32),
        scratch_shapes=[pltpu.VMEM((tm, d), BF16), pltpu.VMEM((tm, d), F32)],
        compiler_params=_params(("arbitrary", "arbitrary")),
        name="ffn_dense",
    )(h2, x, gate[0], wg, wu, wd)


def _ffn_group_kernel(te_ref, na_ref, nv_ref, src_ref, nsrc_ref, dst_ref, h_hbm, wg_ref, wu_ref, wd_ref, y_hbm,
                      xbuf, hb_ref, acc_ref, ybuf, gsem, ssem):
    m = pl.program_id(0)
    j = pl.program_id(1)
    na = na_ref[0]
    tm = hb_ref.shape[0]
    slot = m % 2

    def gather(idx_ref, sl):
        def body(rb, carry):
            for u in range(DMA_UNROLL):
                r = rb * DMA_UNROLL + u
                pltpu.make_async_copy(h_hbm.at[pl.ds(idx_ref[0, r], 1), :], xbuf.at[sl, pl.ds(r, 1), :],
                                      gsem.at[sl]).start(priority=u % 2)
            return carry
        lax.fori_loop(0, tm // DMA_UNROLL, body, 0)

    def gather_wait(sl):
        pltpu.make_async_copy(h_hbm.at[pl.ds(0, tm), :], xbuf.at[sl], gsem.at[sl]).wait()

    def scatter_row(r, row):
        return pltpu.make_async_copy(ybuf.at[pl.ds(r, 1), :], y_hbm.at[pl.ds(row, 1), :], ssem)

    def scatter_wait(count):
        p = tm
        while p >= 1:
            @pl.when((count & p) != 0)
            def _(p=p):
                pltpu.make_async_copy(ybuf.at[pl.ds(0, p), :], y_hbm.at[pl.ds(0, p), :], ssem).wait()
            p //= 2

    @pl.when(m < na)
    def _():
        @pl.when(j == 0)
        def _():
            @pl.when(m == 0)
            def _():
                gather(src_ref, 0)

            @pl.when(m + 1 < na)
            def _():
                gather(nsrc_ref, 1 - slot)

            gather_wait(slot)
            hb_ref[...] = xbuf[slot].astype(BF16)
            acc_ref[...] = jnp.zeros_like(acc_ref)

        _swiglu_step(hb_ref, wg_ref, wu_ref, wd_ref, acc_ref)

        @pl.when(j == pl.num_programs(1) - 1)
        def _():
            @pl.when(m > 0)
            def _():
                scatter_wait(nv_ref[jnp.maximum(m - 1, 0)])

            ybuf[...] = acc_ref[...]

            nv = nv_ref[m]
            groups = lax.shift_right_logical(nv, DMA_UNROLL.bit_length() - 1)

            def body_group(rb, carry):
                for u in range(DMA_UNROLL):
                    r = rb * DMA_UNROLL + u
                    scatter_row(r, dst_ref[0, r]).start(priority=u % 2)
                return carry

            def body_row(r, carry):
                scatter_row(r, dst_ref[0, r]).start()
                return carry

            lax.fori_loop(0, groups, body_group, 0)
            lax.fori_loop(groups * DMA_UNROLL, nv, body_row, 0)

            @pl.when(m == na - 1)
            def _():
                scatter_wait(nv_ref[m])


def _ffn_group_call(tile_expert, n_active, n_valid, src, dst, h_all, wg, wu, wd, li, *, tm, tf):
    n_tiles = tile_expert.shape[0]
    d = h_all.shape[1]
    f = wg.shape[-1]
    idx = lambda fn: pl.BlockSpec((None, 1, tm), lambda m, j, te, na, nv: (fn(m), 0, 0), memory_space=pltpu.SMEM)
    wspec = lambda shape, fn: pl.BlockSpec((None, None) + shape, lambda m, j, te, na, nv: (li, te[m]) + fn(j))
    grid_spec = pltpu.PrefetchScalarGridSpec(
        num_scalar_prefetch=3,
        grid=(n_tiles, f // tf),
        in_specs=[idx(lambda m: m), idx(lambda m: jnp.minimum(m + 1, n_tiles - 1)), idx(lambda m: m),
                  pl.BlockSpec(memory_space=pl.ANY),
                  wspec((d, tf), lambda j: (0, j)), wspec((d, tf), lambda j: (0, j)), wspec((tf, d), lambda j: (j, 0))],
        out_specs=pl.BlockSpec(memory_space=pl.ANY),
        scratch_shapes=[pltpu.VMEM((2, tm, d), F32), pltpu.VMEM((tm, d), BF16), pltpu.VMEM((tm, d), F32),
                        pltpu.VMEM((tm, d), F32), pltpu.SemaphoreType.DMA((2,)), pltpu.SemaphoreType.DMA(())],
    )
    src3 = src.reshape(n_tiles, 1, tm)
    return pl.pallas_call(
        _ffn_group_kernel,
        grid_spec=grid_spec,
        out_shape=jax.ShapeDtypeStruct((TOP_K * (h_all.shape[0]), d), F32),
        compiler_params=_params(("arbitrary", "arbitrary")),
        name="ffn_grouped",
    )(tile_expert, n_active, n_valid, src3, src3, dst.reshape(n_tiles, 1, tm), h_all, wg, wu, wd)


def _combine_kernel(x_ref, gate_ref, rt_ref, y1_ref, y2_ref, o_ref):
    rt = rt_ref[...]
    o_ref[...] = x_ref[...] + gate_ref[...] * (rt[:, 2:3] * y1_ref[...] + rt[:, 3:4] * y2_ref[...])


def _combine_call(x, gate, route, y_slots, *, tok_off, rows_per_batch, tm):
    n, d = x.shape
    tok = lambda w: pl.BlockSpec((tm, w), lambda i: (i, 0))
    slot = lambda s: pl.BlockSpec((None, tm, d), lambda i: (s, tok_off // tm + i, 0))
    return pl.pallas_call(
        _combine_kernel,
        grid=(n // tm,),
        in_specs=[tok(d), _mod_spec(gate, tm, rows_per_batch), tok(LANES), slot(0), slot(1)],
        out_specs=tok(d),
        out_shape=jax.ShapeDtypeStruct((n, d), F32),
        compiler_params=_params(("arbitrary",)),
        name="moe_combine",
    )(x, gate[0], route, y_slots, y_slots)


def _routing_plan(route_all, n_exp, tm):
    n_tok = route_all.shape[0]
    e = jnp.concatenate([route_all[:, 0], route_all[:, 1]]).astype(jnp.int32)
    onehot = (e[:, None] == jnp.arange(n_exp, dtype=jnp.int32)[None, :]).astype(jnp.int32)
    cnt = jnp.sum(onehot, axis=0)
    rank = jnp.sum((jnp.cumsum(onehot, axis=0) - onehot) * onehot, axis=1)
    padded = ((cnt + tm - 1) // tm) * tm
    ends = jnp.cumsum(padded)
    starts = ends - padded
    pos = starts[e] + rank
    n_rows = TOP_K * n_tok + n_exp * tm
    pair = jnp.arange(TOP_K * n_tok, dtype=jnp.int32)
    dst = jnp.zeros((n_rows,), jnp.int32).at[pos].set(pair)
    src = dst % n_tok
    tile_start = jnp.arange(n_rows // tm, dtype=jnp.int32) * tm
    tile_expert = jnp.minimum(jnp.searchsorted(ends, tile_start, side="right"), n_exp - 1).astype(jnp.int32)
    n_valid = jnp.clip((starts + cnt)[tile_expert] - tile_start, 0, tm).astype(jnp.int32)
    n_active = (ends[-1] // tm).astype(jnp.int32).reshape(1)
    return src, dst, tile_expert, n_valid, n_active


def _rope_tables(pos, hd):
    half = hd // 2
    inv = ROPE_THETA ** (-jnp.arange(0, half, dtype=F32) * 2.0 / hd)
    ang = pos.astype(F32)[:, None] * inv[None, :]
    cos = jnp.cos(ang)
    sin = jnp.sin(ang)
    reps = LANES // hd
    return jnp.tile(jnp.concatenate([cos, cos], axis=1), (1, reps)), jnp.tile(jnp.concatenate([-sin, sin], axis=1), (1, reps))


def kernel(x_prompt, x_sample, cache_k_win, cache_v_win, state_conv, c_prompt, c_sample, w_ada, b_ada, norm1, norm2, w_in, q_norm, k_norm, sinks, w_attn_out, conv_w, conv_b, conv_ln_g, conv_ln_b, w_conv_out, w_o, w_ff_gate, w_ff_up, w_ff_down, w_router, b_router, w_e_gate, w_e_up, w_e_down):
    bsz, seq, d = x_prompt.shape
    dec_b, dec_s, _ = x_sample.shape
    depth = w_in.shape[0]
    hd = q_norm.shape[1]
    n_heads = sinks.shape[1]
    win, n_kv = cache_k_win.shape[2], cache_k_win.shape[3]
    kv_w = n_kv * hd
    group = n_heads // n_kv
    cw = conv_w.shape[1]
    n_exp = w_router.shape[-1]
    np_tok, ns_tok = bsz * seq, dec_b * dec_s
    n_tok = np_tok + ns_tok

    c_all = jnp.concatenate([c_prompt, c_sample], axis=0)
    bp = -(-c_all.shape[0] // 8) * 8
    mods = _ada_call(jnp.pad(c_all, ((0, bp - c_all.shape[0]), (0, 0))), w_ada, b_ada)
    mods = mods.reshape(depth, bp, 6, d)
    mods_p = mods[:, :bsz].transpose(0, 2, 1, 3)[:, :, :, None, :]
    mods_s = jnp.repeat(mods[:, bsz:bsz + dec_b], dec_s, axis=1).transpose(0, 2, 1, 3)[:, :, None]
    mod_p = [[(mods_p, l, i) for i in range(6)] for l in range(depth)]
    mod_s = [[(mods_s, l, i) for i in range(6)] for l in range(depth)]

    w_in_bf, w_ao_bf, w_co_bf, w_o_bf = (w.astype(BF16) for w in (w_in, w_attn_out, w_conv_out, w_o))
    w_fg_bf, w_fu_bf, w_fd_bf = (w.astype(BF16) for w in (w_ff_gate, w_ff_up, w_ff_down))
    w_r_hi = w_router.astype(BF16)
    w_r_lo = (w_router - w_r_hi.astype(F32)).astype(BF16)
    w_r_pad = jnp.pad(jnp.concatenate([w_r_hi, w_r_lo], axis=-1), ((0, 0), (0, 0), (0, LANES - 2 * n_exp)))
    b_r3 = jnp.pad(b_router, ((0, 0), (0, LANES - n_exp))).reshape(-1, 1, LANES)

    reps = LANES // hd
    qn2 = jnp.tile(q_norm, (1, reps)).reshape(depth, 1, LANES)
    kn2 = jnp.tile(k_norm, (1, reps)).reshape(depth, 1, LANES)
    cos_p, sin_p = _rope_tables(jnp.arange(seq, dtype=jnp.int32), hd)
    cos_s, sin_s = _rope_tables(jnp.tile(PAST_LEN + jnp.arange(dec_s, dtype=jnp.int32), dec_b), hd)
    r3 = lambda a: a.reshape(depth, 1, a.shape[-1])
    norm1_3, norm2_3, conv_b3, ln_g3, ln_b3 = r3(norm1), r3(norm2), r3(conv_b), r3(conv_ln_g), r3(conv_ln_b)
    cache_k = cache_k_win.reshape(depth, dec_b, win, kv_w)
    cache_v = cache_v_win.reshape(depth, dec_b, win, kv_w)
    head_kv = (jnp.arange(n_heads) // group)
    blockdiag = (head_kv[:, None] == jnp.arange(n_kv)[None, :]).astype(BF16)
    new_pad = 16
    sink_rows = jnp.broadcast_to(jnp.repeat(sinks, dec_s, axis=1)[:, :, None], (depth, n_heads * dec_s, LANES))

    xp = x_prompt.reshape(np_tok, d)
    xs = x_sample.reshape(ns_tok, d)
    outs = {k: [] for k in ("kp", "vp", "cp", "ks", "vs", "cs")}
    tm_moe = 512

    for l in range(depth):
        mp, ms = mod_p[l], mod_s[l]
        q, k, v, u, sga, sgb = _premix_call(xp, mp[0], mp[1], norm1_3, w_in_bf, qn2, kn2, cos_p, sin_p, l,
                                            rows_per_batch=seq, tm=256, hd=hd, kv_w=kv_w)
        o = _attn_prompt_call(q, k, v, sinks[l], rows_per_batch=seq, blk=win, hd=hd, group=group)
        cv = _conv_prompt_call(u, conv_w, conv_b3, ln_g3, ln_b3, l, rows_per_batch=seq, tm=256)
        outs["kp"].append(k.reshape(bsz, seq, kv_w)[:, seq - win:].reshape(bsz, win, n_kv, hd))
        outs["vp"].append(v.reshape(bsz, seq, kv_w)[:, seq - win:].reshape(bsz, win, n_kv, hd))
        outs["cp"].append(u.reshape(bsz, seq, d)[:, seq - (cw - 1):])
        moe_layer = l % 2 == 1
        router = (w_r_pad, b_r3, l // 2) if moe_layer else None
        res_p = _tail_call(o, cv, sga, sgb, xp, mp[2], mp[3], mp[4], norm2_3, w_ao_bf, w_co_bf, w_o_bf, l, router,
                           n_exp=n_exp, rows_per_batch=seq, tm=256)

        q, k, v, u, sga, sgb = _premix_call(xs, ms[0], ms[1], norm1_3, w_in_bf, qn2, kn2, cos_s, sin_s, l,
                                            rows_per_batch=ns_tok, tm=256, hd=hd, kv_w=kv_w)
        q4 = q.reshape(dec_b, dec_s, n_heads, hd).transpose(0, 2, 1, 3)
        qblk = (q4[:, :, :, None, :] * blockdiag[None, :, None, :, None]).reshape(dec_b, n_heads * dec_s, kv_w)
        k3, v3 = k.reshape(dec_b, dec_s, kv_w), v.reshape(dec_b, dec_s, kv_w)
        padn = lambda a: jnp.pad(a, ((0, 0), (0, new_pad - dec_s), (0, 0)))
        o = _attn_sample_call(qblk, cache_k, cache_v, padn(k3), padn(v3), sink_rows, l, dec_seq=dec_s, hd=hd)
        o = o.reshape(dec_b, n_heads, dec_s, hd).transpose(0, 2, 1, 3).reshape(ns_tok, n_heads * hd)
        u3 = u.reshape(dec_b, dec_s, d)
        cv = _conv_sample_call(state_conv, u3, conv_w, conv_b3, ln_g3, ln_b3, l, rows=SUBLANES)[:, :dec_s].reshape(ns_tok, d)
        outs["ks"].append(k3)
        outs["vs"].append(v3)
        outs["cs"].append(u3)
        res_s = _tail_call(o, cv, sga, sgb, xs, ms[2], ms[3], ms[4], norm2_3, w_ao_bf, w_co_bf, w_o_bf, l, router,
                           n_exp=n_exp, rows_per_batch=ns_tok, tm=256)

        if not moe_layer:
            xp, h2p = res_p
            xs, h2s = res_s
            xp = _ffn_dense_call(h2p, xp, mp[5], w_fg_bf, w_fu_bf, w_fd_bf, l // 2, rows_per_batch=seq, tm=512, tf=1408)
            xs = _ffn_dense_call(h2s, xs, ms[5], w_fg_bf, w_fu_bf, w_fd_bf, l // 2, rows_per_batch=ns_tok, tm=512, tf=1408)
        else:
            xp, h2p, rtp = res_p
            xs, h2s, rts = res_s
            h_all = jnp.concatenate([h2p, h2s], axis=0)
            rt_all = jnp.concatenate([rtp, rts], axis=0)
            src, dst, tile_expert, n_valid, n_active = _routing_plan(rt_all, n_exp, tm_moe)
            y = _ffn_group_call(tile_expert, n_active, n_valid, src, dst, h_all, w_e_gate, w_e_up, w_e_down, l // 2,
                                tm=tm_moe, tf=896).reshape(TOP_K, n_tok, d)
            xp = _combine_call(xp, mp[5], rtp, y, tok_off=0, rows_per_batch=seq, tm=256)
            xs = _combine_call(xs, ms[5], rts, y, tok_off=np_tok, rows_per_batch=ns_tok, tm=256)

    new_kv = lambda rows: jnp.stack(rows).reshape(depth, dec_b, dec_s, n_kv, hd)
    k_win_s = jnp.concatenate([cache_k_win[:, :, dec_s:], new_kv(outs["ks"])], axis=2)
    v_win_s = jnp.concatenate([cache_v_win[:, :, dec_s:], new_kv(outs["vs"])], axis=2)
    conv_s = jnp.concatenate([state_conv[:, :, dec_s:], jnp.stack(outs["cs"])], axis=2)
    return (xp.reshape(bsz, seq, d), xs.reshape(dec_b, dec_s, d),
            jnp.stack(outs["kp"]), jnp.stack(outs["vp"]), jnp.stack(outs["cp"]),
            k_win_s, v_win_s, conv_s)
```
